```python
import math, functools
import jax, jax.numpy as jnp
from jax import lax
import numpy as np

D_MODEL = 1024
BATCH = 8
SEQ = 2048
DEPTH = 4
DEC_BATCH = 32
DEC_SEQ = 1
PAST_LEN = 16384
PAGE_SIZE = 128

N_MIXERS = 2
N_MLA_LAYERS = (DEPTH + 1) // 2
N_DIL_LAYERS = DEPTH // 2
MLA_HEADS = 16
MLA_NOPE = 64
MLA_ROPE = 32
MLA_VDIM = 64
MLA_Q_LORA = 384
MLA_KV_LORA = 256
ROPE_THETA = 10000.0
MLA_BLOCK_Q = 128
DIL_GROUPS = ((128, 1), (512, 4), (2048, 16))
DIL_HEADS = 8
DIL_HEAD_DIM = 64
D_FF = 4 * D_MODEL
EPS = 1e-6
NEG_INF = -1e30

kernel_name = "mla_dilated_swa_hybrid_step"


def rmsnorm(x, g):
    xf = x.astype(jnp.float32)
    y = xf * lax.rsqrt(jnp.mean(xf * xf, axis=-1, keepdims=True) + EPS)
    return (y * g.astype(jnp.float32)).astype(x.dtype)


def rope(x, pos):
    half = MLA_ROPE // 2
    inv = ROPE_THETA ** (-jnp.arange(half, dtype=jnp.float32) / half)
    ang = pos.astype(jnp.float32)[:, None] * inv[None, :]
    cos = jnp.cos(ang)[None, :, None, :]
    sin = jnp.sin(ang)[None, :, None, :]
    xf = x.astype(jnp.float32)
    x1, x2 = xf[..., :half], xf[..., half:]
    return jnp.concatenate([x1 * cos - x2 * sin, x1 * sin + x2 * cos], axis=-1).astype(x.dtype)


def sqrelu_ffn(h, w_up, w_down):
    return jnp.square(jax.nn.relu(h @ w_up)) @ w_down


def mla_project(x, pos, w_dq, g_q, w_uq, w_dkv, g_kv):
    B, T, _ = x.shape
    c_q = rmsnorm(x @ w_dq, g_q)
    q = (c_q @ w_uq).reshape(B, T, MLA_HEADS, MLA_NOPE + MLA_ROPE)
    q_nope = q[..., :MLA_NOPE]
    q_pe = rope(q[..., MLA_NOPE:], pos)
    kv_a = x @ w_dkv
    c_kv = rmsnorm(kv_a[..., :MLA_KV_LORA], g_kv)
    k_pe = rope(kv_a[..., None, MLA_KV_LORA:], pos)[:, :, 0]
    return q_nope, q_pe, c_kv, k_pe


def mla_prompt(x, w_dq, g_q, w_uq, w_dkv, g_kv, w_uk, w_uv, w_o):
    B, S, _ = x.shape
    pos = jnp.arange(S)
    q_nope, q_pe, c_kv, k_pe = mla_project(x, pos, w_dq, g_q, w_uq, w_dkv, g_kv)
    k_nope = jnp.einsum("bsc,chn->bshn", c_kv, w_uk)
    v = jnp.einsum("bsc,chv->bshv", c_kv, w_uv)
    nb = S // MLA_BLOCK_Q
    qn_b = jnp.moveaxis(q_nope.reshape(B, nb, MLA_BLOCK_Q, MLA_HEADS, MLA_NOPE), 1, 0)
    qp_b = jnp.moveaxis(q_pe.reshape(B, nb, MLA_BLOCK_Q, MLA_HEADS, MLA_ROPE), 1, 0)
    scale = (MLA_NOPE + MLA_ROPE) ** -0.5
    key_pos = jnp.arange(S)

    def block(args):
        qn, qp, q0 = args
        s = jnp.einsum("bqhn,bkhn->bhqk", qn, k_nope) + jnp.einsum("bqhr,bkr->bhqk", qp, k_pe)
        s = s.astype(jnp.float32) * scale
        qpos = q0 + jnp.arange(MLA_BLOCK_Q)
        s = jnp.where(key_pos[None, :] <= qpos[:, None], s, NEG_INF)
        p = jax.nn.softmax(s, axis=-1).astype(v.dtype)
        return jnp.einsum("bhqk,bkhv->bqhv", p, v)

    o = lax.map(block, (qn_b, qp_b, jnp.arange(nb) * MLA_BLOCK_Q))
    o = jnp.moveaxis(o, 0, 1).reshape(B, S, MLA_HEADS * MLA_VDIM)
    return o @ w_o, c_kv, k_pe


def mla_sample(x, ckv_pool, kpe_pool, page_table, w_dq, g_q, w_uq, w_dkv, g_kv, w_uk, w_uv, w_o):
    B, T, _ = x.shape
    past = page_table.shape[1] * PAGE_SIZE
    pos = past + jnp.arange(T)
    q_nope, q_pe, c_kv, k_pe = mla_project(x, pos, w_dq, g_q, w_uq, w_dkv, g_kv)
    ckv_past = ckv_pool[page_table].reshape(B, past, MLA_KV_LORA)
    kpe_past = kpe_pool[page_table].reshape(B, past, MLA_ROPE)
    q_lat = jnp.einsum("bthn,chn->bthc", q_nope, w_uk)
    scale = (MLA_NOPE + MLA_ROPE) ** -0.5
    s_past = jnp.einsum("bthc,bkc->bhtk", q_lat, ckv_past) + jnp.einsum("bthr,bkr->bhtk", q_pe, kpe_past)
    s_new = jnp.einsum("bthc,bkc->bhtk", q_lat, c_kv) + jnp.einsum("bthr,bkr->bhtk", q_pe, k_pe)
    causal = jnp.tril(jnp.ones((T, T), dtype=bool))
    s_new = jnp.where(causal, s_new.astype(jnp.float32), NEG_INF)
    s = jnp.concatenate([s_past.astype(jnp.float32), s_new], axis=-1) * scale
    p = jax.nn.softmax(s, axis=-1).astype(c_kv.dtype)
    o_lat = jnp.einsum("bhtk,bkc->bthc", p[..., :past], ckv_past) + jnp.einsum("bhtk,bkc->bthc", p[..., past:], c_kv)
    o = jnp.einsum("bthc,chv->bthv", o_lat, w_uv).reshape(B, T, MLA_HEADS * MLA_VDIM)
    return o @ w_o, c_kv, k_pe


def alibi_slopes():
    return jnp.power(2.0, -8.0 * jnp.arange(1, DIL_HEADS + 1, dtype=jnp.float32) / DIL_HEADS)


def _to_sub(a, d):
    B, S = a.shape[:2]
    rest = a.shape[2:]
    a = a.reshape((B, S // d, d) + rest)
    return jnp.moveaxis(a, 2, 1).reshape((B * d, S // d) + rest)


def _from_sub(a, B, d, L):
    a = a[:, :L]
    rest = a.shape[2:]
    a = jnp.moveaxis(a.reshape((B, d, L) + rest), 1, 2)
    return a.reshape((B, L * d) + rest)


def dilated_group_prompt(q, k, v, dilation, span, slopes):
    B, S, H, E = q.shape
    L = S // dilation
    nb = -(-L // span)
    pad = nb * span - L
    z = B * dilation
    qs, ks, vs = _to_sub(q, dilation), _to_sub(k, dilation), _to_sub(v, dilation)
    qb = jnp.pad(qs, ((0, 0), (0, pad), (0, 0), (0, 0))).reshape(z, nb, span, H, E)

    def band(a):
        ap = jnp.pad(a, ((0, 0), (span, pad), (0, 0), (0, 0))).reshape(z, nb + 1, span, H, E)
        return jnp.concatenate([ap[:, :-1], ap[:, 1:]], axis=2)

    kb, vb = band(ks), band(vs)
    qi = jnp.arange(span)[:, None]
    ki = jnp.arange(2 * span)[None, :]
    dist = qi + span - ki
    key_sub = (jnp.arange(nb) * span - span)[:, None, None] + ki[None]
    valid = (dist >= 0) & (dist <= span) & (key_sub >= 0)
    scale = DIL_HEAD_DIM ** -0.5
    s = jnp.einsum("znqhe,znkhe->znhqk", qb, kb).astype(jnp.float32) * scale
    s = s - slopes[:, None, None] * (dist * dilation).astype(jnp.float32)
    s = jnp.where(valid[None, :, None], s, NEG_INF)
    m = jnp.max(s, axis=-1)
    e = jnp.exp(s - m[..., None])
    den = jnp.sum(e, axis=-1)
    o = jnp.einsum("znhqk,znkhe->znqhe", e, vb.astype(jnp.float32)).reshape(z, nb * span, H, E)
    m = jnp.swapaxes(m, 2, 3).reshape(z, nb * span, H)
    den = jnp.swapaxes(den, 2, 3).reshape(z, nb * span, H)
    return _from_sub(o, B, dilation, L), _from_sub(m, B, dilation, L), _from_sub(den, B, dilation, L)


def dilated_group_sample(q, k, v, buf, window, dilation, span, slopes):
    B, T, H, E = q.shape
    WB = buf.shape[1]
    kv_all = jnp.concatenate([buf, jnp.stack([k, v], axis=2).astype(buf.dtype)], axis=1)
    idx = WB + jnp.arange(T)[:, None] - dilation * jnp.arange(span + 1)[None, :]
    valid = idx >= 0
    g = kv_all[:, jnp.maximum(idx, 0)]
    scale = DIL_HEAD_DIM ** -0.5
    s = jnp.einsum("bthe,btihe->bhti", q, g[:, :, :, 0]).astype(jnp.float32) * scale
    s = s - slopes[:, None, None] * (dilation * jnp.arange(span + 1)).astype(jnp.float32)
    s = jnp.where(valid[None, None], s, NEG_INF)
    m = jnp.max(s, axis=-1)
    e = jnp.exp(s - m[..., None])
    den = jnp.sum(e, axis=-1)
    o = jnp.einsum("bhti,btihe->bthe", e, g[:, :, :, 1].astype(jnp.float32))
    keep = min(window, WB + T)
    new_buf = kv_all[:, WB + T - keep:]
    return o, jnp.swapaxes(m, 1, 2), jnp.swapaxes(den, 1, 2), new_buf


def merge_groups(parts):
    M = functools.reduce(jnp.maximum, [m for _, m, _ in parts])
    num = sum(o * jnp.exp(m - M)[..., None] for o, m, _ in parts)
    den = sum(dn * jnp.exp(m - M) for _, m, dn in parts)
    return num / den[..., None]


def dil_prompt(x, w_qkv, w_o):
    B, S, _ = x.shape
    qkv = (x @ w_qkv).reshape(B, S, len(DIL_GROUPS), 3, DIL_HEADS, DIL_HEAD_DIM)
    slopes = alibi_slopes()
    parts, bufs = [], []
    for gi, (window, dilation) in enumerate(DIL_GROUPS):
        q, k, v = qkv[:, :, gi, 0], qkv[:, :, gi, 1], qkv[:, :, gi, 2]
        parts.append(dilated_group_prompt(q, k, v, dilation, window // dilation, slopes))
        bufs.append(jnp.stack([k, v], axis=2)[:, S - min(window, S):])
    o = merge_groups(parts).astype(x.dtype).reshape(B, S, DIL_HEADS * DIL_HEAD_DIM)
    return o @ w_o, bufs


def dil_sample(x, bufs_in, w_qkv, w_o):
    B, T, _ = x.shape
    qkv = (x @ w_qkv).reshape(B, T, len(DIL_GROUPS), 3, DIL_HEADS, DIL_HEAD_DIM)
    slopes = alibi_slopes()
    parts, bufs = [], []
    for gi, (window, dilation) in enumerate(DIL_GROUPS):
        q, k, v = qkv[:, :, gi, 0], qkv[:, :, gi, 1], qkv[:, :, gi, 2]
        o, m, dn, nbuf = dilated_group_sample(q, k, v, bufs_in[gi], window, dilation, window // dilation, slopes)
        parts.append((o, m, dn))
        bufs.append(nbuf)
    o = merge_groups(parts).astype(x.dtype).reshape(B, T, DIL_HEADS * DIL_HEAD_DIM)
    return o @ w_o, bufs


def setup_inputs(seed: int = 0) -> dict:
    key = jax.random.key(seed)
    ks = jax.random.split(key, 24)
    f32 = jnp.float32
    n_pages = PAST_LEN // PAGE_SIZE
    n_used = DEC_BATCH * n_pages
    n_pool = n_used + n_used // 4

    def nrm(k, shape, scale=1.0):
        return jax.random.normal(k, shape, f32) * scale

    def gain(k, shape):
        return 1.0 + 0.01 * jax.random.normal(k, shape, f32)

    page_table = jax.random.permutation(ks[0], n_pool)[:n_used].reshape(DEC_BATCH, n_pages).astype(jnp.int32)
    dil_states = [nrm(ks[1 + gi], (N_DIL_LAYERS, DEC_BATCH, min(w, PAST_LEN), 2, DIL_HEADS, DIL_HEAD_DIM))
                  for gi, (w, d) in enumerate(DIL_GROUPS)]
    return {
        "x_prompt": nrm(ks[4], (BATCH, SEQ, D_MODEL)),
        "x_sample": nrm(ks[5], (DEC_BATCH, DEC_SEQ, D_MODEL)),
        "cache_mla_ckv": nrm(ks[6], (N_MLA_LAYERS, n_pool, PAGE_SIZE, MLA_KV_LORA)),
        "cache_mla_kpe": nrm(ks[7], (N_MLA_LAYERS, n_pool, PAGE_SIZE, MLA_ROPE)),
        "page_table": page_table,
        "state_dil_kv_w128": dil_states[0],
        "state_dil_kv_w512": dil_states[1],
        "state_dil_kv_w2048": dil_states[2],
        "mla_w_dq": nrm(ks[8], (N_MLA_LAYERS, D_MODEL, MLA_Q_LORA), D_MODEL ** -0.5),
        "mla_g_q": gain(ks[9], (N_MLA_LAYERS, MLA_Q_LORA)),
        "mla_w_uq": nrm(ks[10], (N_MLA_LAYERS, MLA_Q_LORA, MLA_HEADS * (MLA_NOPE + MLA_ROPE)), MLA_Q_LORA ** -0.5),
        "mla_w_dkv": nrm(ks[11], (N_MLA_LAYERS, D_MODEL, MLA_KV_LORA + MLA_ROPE), D_MODEL ** -0.5),
        "mla_g_kv": gain(ks[12], (N_MLA_LAYERS, MLA_KV_LORA)),
        "mla_w_uk": nrm(ks[13], (N_MLA_LAYERS, MLA_KV_LORA, MLA_HEADS, MLA_NOPE), MLA_KV_LORA ** -0.5),
        "mla_w_uv": nrm(ks[14], (N_MLA_LAYERS, MLA_KV_LORA, MLA_HEADS, MLA_VDIM), MLA_KV_LORA ** -0.5),
        "mla_w_o": nrm(ks[15], (N_MLA_LAYERS, MLA_HEADS * MLA_VDIM, D_MODEL), (MLA_HEADS * MLA_VDIM) ** -0.5),
        "dil_w_qkv": nrm(ks[16], (N_DIL_LAYERS, D_MODEL, len(DIL_GROUPS) * 3 * DIL_HEADS * DIL_HEAD_DIM), D_MODEL ** -0.5),
        "dil_w_o": nrm(ks[17], (N_DIL_LAYERS, DIL_HEADS * DIL_HEAD_DIM, D_MODEL), (DIL_HEADS * DIL_HEAD_DIM) ** -0.5),
        "norm_mix": gain(ks[18], (DEPTH, D_MODEL)),
        "norm_ffn": gain(ks[19], (DEPTH, D_MODEL)),
        "ffn_w_up": nrm(ks[20], (DEPTH, D_MODEL, D_FF), D_MODEL ** -0.5),
        "ffn_w_down": nrm(ks[21], (DEPTH, D_FF, D_MODEL), 0.5 * D_FF ** -0.5),
        "norm_final": gain(ks[22], (D_MODEL,)),
    }


def reference(x_prompt, x_sample, cache_mla_ckv, cache_mla_kpe, page_table,
              state_dil_kv_w128, state_dil_kv_w512, state_dil_kv_w2048,
              mla_w_dq, mla_g_q, mla_w_uq, mla_w_dkv, mla_g_kv, mla_w_uk, mla_w_uv, mla_w_o,
              dil_w_qkv, dil_w_o, norm_mix, norm_ffn, ffn_w_up, ffn_w_down, norm_final):
    dil_states = (state_dil_kv_w128, state_dil_kv_w512, state_dil_kv_w2048)
    xp, xs = x_prompt, x_sample
    ckv_p, kpe_p, ckv_s, kpe_s = [], [], [], []
    dil_p = [[] for _ in DIL_GROUPS]
    dil_s = [[] for _ in DIL_GROUPS]
    for layer in range(DEPTH):
        i = layer // N_MIXERS
        hp = rmsnorm(xp, norm_mix[layer])
        hs = rmsnorm(xs, norm_mix[layer])
        if layer % N_MIXERS == 0:
            w = (mla_w_dq[i], mla_g_q[i], mla_w_uq[i], mla_w_dkv[i], mla_g_kv[i], mla_w_uk[i], mla_w_uv[i], mla_w_o[i])
            yp, c_p, k_p = mla_prompt(hp, *w)
            ys, c_s, k_s = mla_sample(hs, cache_mla_ckv[i], cache_mla_kpe[i], page_table, *w)
            ckv_p.append(c_p); kpe_p.append(k_p); ckv_s.append(c_s); kpe_s.append(k_s)
        else:
            yp, bp = dil_prompt(hp, dil_w_qkv[i], dil_w_o[i])
            ys, bs = dil_sample(hs, [st[i] for st in dil_states], dil_w_qkv[i], dil_w_o[i])
            for gi in range(len(DIL_GROUPS)):
                dil_p[gi].append(bp[gi]); dil_s[gi].append(bs[gi])
        xp = xp + yp
        xs = xs + ys
        xp = xp + sqrelu_ffn(rmsnorm(xp, norm_ffn[layer]), ffn_w_up[layer], ffn_w_down[layer])
        xs = xs + sqrelu_ffn(rmsnorm(xs, norm_ffn[layer]), ffn_w_up[layer], ffn_w_down[layer])
    y_prompt = rmsnorm(xp, norm_final)
    y_sample = rmsnorm(xs, norm_final)
    return (y_prompt, y_sample,
            jnp.stack(ckv_p), jnp.stack(kpe_p), jnp.stack(ckv_s), jnp.stack(kpe_s),
            jnp.stack(dil_p[0]), jnp.stack(dil_p[1]), jnp.stack(dil_p[2]),
            jnp.stack(dil_s[0]), jnp.stack(dil_s[1]), jnp.stack(dil_s[2]))
```

```python
import functools

import jax
import jax.numpy as jnp
from jax import lax
from jax.experimental import pallas as pl
from jax.experimental.pallas import tpu as pltpu

F32 = jnp.float32
BF16 = jnp.bfloat16

D_MODEL = 1024
BATCH = 8
SEQ = 2048
DEPTH = 4
DEC_BATCH = 32
PAST_LEN = 16384
PAGE_SIZE = 128
N_PAGES = PAST_LEN // PAGE_SIZE
MLA_HEADS = 16
MLA_NOPE = 64
MLA_ROPE = 32
MLA_VDIM = 64
MLA_Q_LORA = 384
MLA_KV_LORA = 256
ROPE_THETA = 10000.0
DIL_GROUPS = ((128, 1), (512, 4), (2048, 16))
DIL_SPAN = 128
DIL_HEADS = 8
DIL_HEAD_DIM = 64
DIL_WIDTH = DIL_HEADS * DIL_HEAD_DIM
D_FF = 4 * D_MODEL
EPS = 1e-6
NEG_INF = -1e30

LANES = 128
N_PROMPT = BATCH * SEQ
HEAD_PAD = MLA_HEADS * LANES
VMEM_LIMIT = 56 * 1024 * 1024
PAGES_PER_STEP = 16


def _cparams(sem):
    return pltpu.CompilerParams(dimension_semantics=sem, vmem_limit_bytes=VMEM_LIMIT)


def _rms(x, g):
    return x * lax.rsqrt(jnp.mean(x * x, axis=-1, keepdims=True) + EPS) * g


def _const_spec(shape):
    nd = len(shape)
    return pl.BlockSpec(shape, lambda *_: (0,) * nd, pipeline_mode=pl.Buffered(1))


def _mla_proj_kernel(*refs, with_kv):
    if with_kv:
        (x_ref, gmix_ref, wa_ref, gq_ref, wq1_ref, wq2_ref, gkv_ref, cq_ref, sq_ref, ck_ref, sk_ref,
         wk_ref, wv_ref, q_out, ckv_out, kpe_out, k_out, v_out) = refs
    else:
        (x_ref, gmix_ref, wa_ref, gq_ref, wq1_ref, wq2_ref, gkv_ref, cq_ref, sq_ref, ck_ref, sk_ref,
         q_out, ckv_out, kpe_out) = refs
    h = _rms(x_ref[...], gmix_ref[...]).astype(BF16)
    a = jnp.dot(h, wa_ref[...], preferred_element_type=F32)
    c_q = _rms(a[:, :MLA_Q_LORA], gq_ref[...]).astype(BF16)
    o1 = MLA_Q_LORA + MLA_KV_LORA
    c_kv = _rms(a[:, MLA_Q_LORA:o1], gkv_ref[...])
    k_pe = a[:, o1:o1 + LANES] * ck_ref[...] + a[:, o1 + LANES:o1 + 2 * LANES] * sk_ref[...]
    ckv_out[...] = c_kv
    kpe_out[...] = k_pe[:, :MLA_ROPE]
    cq_t = cq_ref[...]
    sq_t = sq_ref[...]
    heads_per_chunk = 4
    cw = heads_per_chunk * LANES
    cq_c = jnp.concatenate([cq_t] * heads_per_chunk, axis=1)
    sq_c = jnp.concatenate([sq_t] * heads_per_chunk, axis=1)
    for c in range(MLA_HEADS // heads_per_chunk):
        q1 = jnp.dot(c_q, wq1_ref[:, c * cw:(c + 1) * cw], preferred_element_type=F32)
        q2 = jnp.dot(c_q, wq2_ref[:, c * cw:(c + 1) * cw], preferred_element_type=F32)
        q_out[:, c * cw:(c + 1) * cw] = (q1 * cq_c + q2 * sq_c).astype(q_out.dtype)
    if with_kv:
        c_kv_b = c_kv.astype(BF16)
        lhs = jnp.concatenate([c_kv_b, k_pe.astype(BF16)], axis=1)
        k_out[...] = jnp.dot(lhs, wk_ref[...], preferred_element_type=F32).astype(k_out.dtype)
        v_out[...] = jnp.dot(c_kv_b, wv_ref[...], preferred_element_type=F32).astype(v_out.dtype)


def _mla_proj(x, gmix, wa, gq, wq1, wq2, gkv, tabs, wk, wv, *, tm, with_kv):
    n = x.shape[0]
    nt = n // tm
    tab_blocks = tabs[0].shape[0] // tm
    row = lambda w: pl.BlockSpec((tm, w), lambda i: (i, 0))
    tab = pl.BlockSpec((tm, LANES), lambda i: (i % tab_blocks, 0))
    in_specs = [row(D_MODEL), _const_spec(gmix.shape), _const_spec(wa.shape), _const_spec(gq.shape),
                _const_spec(wq1.shape), _const_spec(wq2.shape), _const_spec(gkv.shape), tab, tab, tab, tab]
    args = [x, gmix, wa, gq, wq1, wq2, gkv, *tabs]
    out_shape = [jax.ShapeDtypeStruct((n, HEAD_PAD), BF16), jax.ShapeDtypeStruct((n, MLA_KV_LORA), F32),
                 jax.ShapeDtypeStruct((n, MLA_ROPE), F32)]
    out_specs = [row(HEAD_PAD), row(MLA_KV_LORA), row(MLA_ROPE)]
    if with_kv:
        in_specs += [_const_spec(wk.shape), _const_spec(wv.shape)]
        args += [wk, wv]
        out_shape += [jax.ShapeDtypeStruct((n, HEAD_PAD), BF16),
                      jax.ShapeDtypeStruct((n, MLA_HEADS * MLA_VDIM), BF16)]
        out_specs += [row(HEAD_PAD), row(MLA_HEADS * MLA_VDIM)]
    return pl.pallas_call(
        functools.partial(_mla_proj_kernel, with_kv=with_kv),
        grid=(nt,), in_specs=in_specs, out_specs=out_specs, out_shape=out_shape,
        compiler_params=_cparams(("parallel",)), name="mla_proj_kv" if with_kv else "mla_proj_q",
    )(*args)


def _mla_attn_kernel(q_ref, k_ref, v_ref, o_ref, *, tq):
    i = pl.program_id(2)
    row = lax.broadcasted_iota(jnp.int32, (tq, tq), 0)
    col = lax.broadcasted_iota(jnp.int32, (tq, tq), 1)
    outs = []
    for hh in range(2):
        q = q_ref[:, hh * LANES:(hh + 1) * LANES]

        def step(kb, carry, masked, hh=hh, q=q):
            m, l, acc = carry
            start = pl.multiple_of(kb * tq, tq)
            k = k_ref[pl.ds(start, tq), hh * LANES:(hh + 1) * LANES]
            s = lax.dot_general(q, k, (((1,), (1,)), ((), ())), preferred_element_type=F32)
            if masked:
                s = jnp.where(col <= row, s, NEG_INF)
            m_new = jnp.maximum(m, jnp.max(s, axis=-1, keepdims=True))
            p = jnp.exp(s - m_new)
            alpha = jnp.exp(m - m_new)
            l = alpha * l + jnp.sum(p, axis=-1, keepdims=True)
            v = v_ref[pl.ds(start, tq), :]
            acc = alpha * acc + jnp.dot(p.astype(BF16), v, preferred_element_type=F32)
            return m_new, l, acc

        init = (jnp.full((tq, 1), NEG_INF, F32), jnp.zeros((tq, 1), F32), jnp.zeros((tq, LANES), F32))
        carry = lax.fori_loop(0, i, functools.partial(step, masked=False), init)
        _, l, acc = step(i, carry, True)
        outs.append(acc / l)
    lane = lax.broadcasted_iota(jnp.int32, (tq, LANES), 1)
    o_ref[...] = jnp.where(lane < MLA_VDIM, outs[0], outs[1]).astype(o_ref.dtype)


def _mla_attn(q, k, v, *, tq):
    nq = SEQ // tq
    pairs = MLA_HEADS // 2
    return pl.pallas_call(
        functools.partial(_mla_attn_kernel, tq=tq),
        grid=(BATCH, pairs, nq),
        in_specs=[pl.BlockSpec((tq, 2 * LANES), lambda b, j, i: (b * nq + i, j)),
                  pl.BlockSpec((SEQ, 2 * LANES), lambda b, j, i: (b, j)),
                  pl.BlockSpec((SEQ, LANES), lambda b, j, i: (b, j))],
        out_specs=pl.BlockSpec((tq, LANES), lambda b, j, i: (b * nq + i, j)),
        out_shape=jax.ShapeDtypeStruct((N_PROMPT, MLA_HEADS * MLA_VDIM), BF16),
        compiler_params=_cparams(("parallel", "parallel", "arbitrary")), name="mla_attn_prompt",
    )(q, k, v)


def _mla_qlat_kernel(q_ref, w_ref, o_ref):
    for h in range(MLA_HEADS):
        o_ref[h] = jnp.dot(q_ref[:, h * LANES:(h + 1) * LANES], w_ref[h], preferred_element_type=F32)


def _mla_qlat(q_s, w_qlat):
    return pl.pallas_call(
        _mla_qlat_kernel,
        out_shape=jax.ShapeDtypeStruct((MLA_HEADS, DEC_BATCH, MLA_KV_LORA), F32),
        compiler_params=pltpu.CompilerParams(vmem_limit_bytes=VMEM_LIMIT), name="mla_qlat",
    )(q_s, w_qlat)


def _mla_sample_attn_kernel(pt_ref, qlat_ref, qpe_ref, cnew_ref, knew_ref, *rest):
    del pt_ref
    npg = PAGES_PER_STEP
    ckv_refs = rest[:npg]
    kpe_refs = rest[npg:2 * npg]
    o_ref, m_s, l_s, acc_s = rest[2 * npg:]
    c = pl.program_id(1)
    qlat = qlat_ref[...]
    qpe = qpe_ref[...]

    @pl.when(c == 0)
    def _():
        cnew = cnew_ref[...]
        s_new = (jnp.sum(qlat * cnew, axis=-1, keepdims=True)
                 + jnp.sum(qpe * knew_ref[...], axis=-1, keepdims=True))
        m_s[...] = s_new
        l_s[...] = jnp.ones_like(s_new)
        acc_s[...] = jnp.broadcast_to(cnew, acc_s.shape)

    ckv = jnp.concatenate([r[...] for r in ckv_refs], axis=0).astype(BF16)
    kpe = jnp.concatenate([r[...] for r in kpe_refs], axis=0).astype(BF16)
    nt = (((1,), (1,)), ((), ()))
    s = (lax.dot_general(qlat.astype(BF16), ckv, nt, preferred_element_type=F32)
         + lax.dot_general(qpe.astype(BF16), kpe, nt, preferred_element_type=F32))
    m = m_s[...]
    m_new = jnp.maximum(m, jnp.max(s, axis=-1, keepdims=True))
    p = jnp.exp(s - m_new)
    alpha = jnp.exp(m - m_new)
    l_s[...] = alpha * l_s[...] + jnp.sum(p, axis=-1, keepdims=True)
    acc_s[...] = alpha * acc_s[...] + jnp.dot(p.astype(BF16), ckv, preferred_element_type=F32)
    m_s[...] = m_new

    @pl.when(c == pl.num_programs(1) - 1)
    def _():
        o_ref[...] = acc_s[...] / l_s[...]


def _mla_sample_attn(page_table, qlat, qpe, cnew, knew, ckv_pool, kpe_pool):
    npg = PAGES_PER_STEP
    nchunk = N_PAGES // npg
    per_b = lambda w, r: pl.BlockSpec((None, r, w), lambda b, c, pt: (b, 0, 0))
    page = lambda w, k: pl.BlockSpec((None, PAGE_SIZE, w), lambda b, c, pt, k=k: (pt[b, c * npg + k], 0, 0))
    in_specs = ([per_b(MLA_KV_LORA, MLA_HEADS), per_b(MLA_ROPE, MLA_HEADS), per_b(MLA_KV_LORA, 1), per_b(MLA_ROPE, 1)]
                + [page(MLA_KV_LORA, k) for k in range(npg)] + [page(MLA_ROPE, k) for k in range(npg)])
    grid_spec = pltpu.PrefetchScalarGridSpec(
        num_scalar_prefetch=1, grid=(DEC_BATCH, nchunk), in_specs=in_specs,
        out_specs=pl.BlockSpec((None, MLA_HEADS, MLA_KV_LORA), lambda b, c, pt: (b, 0, 0)),
        scratch_shapes=[pltpu.VMEM((MLA_HEADS, 1), F32), pltpu.VMEM((MLA_HEADS, 1), F32),
                        pltpu.VMEM((MLA_HEADS, MLA_KV_LORA), F32)])
    return pl.pallas_call(
        _mla_sample_attn_kernel, grid_spec=grid_spec,
        out_shape=jax.ShapeDtypeStruct((DEC_BATCH, MLA_HEADS, MLA_KV_LORA), F32),
        compiler_params=_cparams(("parallel", "arbitrary")), name="mla_attn_sample",
    )(page_table, qlat, qpe, cnew, knew, *([ckv_pool] * npg), *([kpe_pool] * npg))


def _mla_sample_out_kernel(olat_ref, w_ref, o_ref):
    for j in range(MLA_HEADS // 2):
        acc = None
        for h in (2 * j, 2 * j + 1):
            t = jnp.dot(olat_ref[h].astype(BF16), w_ref[h], preferred_element_type=F32)
            acc = t if acc is None else acc + t
        o_ref[:, j * LANES:(j + 1) * LANES] = acc.astype(o_ref.dtype)


def _mla_sample_out(olat_t, w_vpair):
    return pl.pallas_call(
        _mla_sample_out_kernel,
        out_shape=jax.ShapeDtypeStruct((DEC_BATCH, MLA_HEADS * MLA_VDIM), BF16),
        compiler_params=pltpu.CompilerParams(vmem_limit_bytes=VMEM_LIMIT), name="mla_sample_out",
    )(olat_t, w_vpair)


def _post_ffn_kernel(*refs, final, ff_chunk):
    if final:
        x_ref, o_ref, wo_ref, g_ref, wup_ref, wdn_ref, gfin_ref, out_ref = refs
    else:
        x_ref, o_ref, wo_ref, g_ref, wup_ref, wdn_ref, out_ref = refs
    x1 = x_ref[...] + jnp.dot(o_ref[...], wo_ref[...], preferred_element_type=F32)
    hn = _rms(x1, g_ref[...]).astype(BF16)
    acc = x1
    for c in range(D_FF // ff_chunk):
        u = jnp.dot(hn, wup_ref[:, c * ff_chunk:(c + 1) * ff_chunk], preferred_element_type=F32)
        u = jnp.square(jnp.maximum(u, 0.0)).astype(BF16)
        acc = acc + jnp.dot(u, wdn_ref[c * ff_chunk:(c + 1) * ff_chunk, :], preferred_element_type=F32)
    if final:
        acc = _rms(acc, gfin_ref[...])
    out_ref[...] = acc


def _post_ffn(x, o, wo, g, wup, wdn, gfin, *, tm, final):
    n = x.shape[0]
    ko = o.shape[1]
    row = lambda w: pl.BlockSpec((tm, w), lambda i: (i, 0))
    in_specs = [row(D_MODEL), row(ko), _const_spec(wo.shape), _const_spec(g.shape),
                _const_spec(wup.shape), _const_spec(wdn.shape)]
    args = [x, o, wo, g, wup, wdn]
    if final:
        in_specs.append(_const_spec(gfin.shape))
        args.append(gfin)
    return pl.pallas_call(
        functools.partial(_post_ffn_kernel, final=final, ff_chunk=1024),
        grid=(n // tm,), in_specs=in_specs, out_specs=row(D_MODEL),
        out_shape=jax.ShapeDtypeStruct((n, D_MODEL), F32),
        compiler_params=_cparams(("parallel",)), name="post_ffn_final" if final else "post_ffn",
    )(*args)


def _dil_proj_kernel(x_ref, g_ref, w_ref, q_out, kv_out):
    h = _rms(x_ref[...], g_ref[...]).astype(BF16)
    gw = 3 * DIL_WIDTH
    for g in range(len(DIL_GROUPS)):
        r = jnp.dot(h, w_ref[:, g * gw:(g + 1) * gw], preferred_element_type=F32)
        q_out[g] = r[:, :DIL_WIDTH] * (DIL_HEAD_DIM ** -0.5)
        kv_out[g] = r[:, DIL_WIDTH:]


def _dil_proj(x, g, w, *, tm):
    n = x.shape[0]
    ng = len(DIL_GROUPS)
    return pl.pallas_call(
        _dil_proj_kernel, grid=(n // tm,),
        in_specs=[pl.BlockSpec((tm, D_MODEL), lambda i: (i, 0)), _const_spec(g.shape), _const_spec(w.shape)],
        out_specs=[pl.BlockSpec((ng, tm, DIL_WIDTH), lambda i: (0, i, 0)),
                   pl.BlockSpec((ng, tm, 2 * DIL_WIDTH), lambda i: (0, i, 0))],
        out_shape=[jax.ShapeDtypeStruct((ng, n, DIL_WIDTH), F32), jax.ShapeDtypeStruct((ng, n, 2 * DIL_WIDTH), F32)],
        compiler_params=_cparams(("parallel",)), name="dil_proj",
    )(x, g, w)


def _strided(start, size, stride):
    return pl.ds(start, size) if stride == 1 else pl.ds(start, size, stride=stride)


def _dil_attn_kernel(q_ref, k_ref, v_ref, o_ref, os_ref, ms_ref, ds_ref):
    j = pl.program_id(1)
    sp = DIL_SPAN
    lane = lax.broadcasted_iota(jnp.int32, (sp, LANES), 1)
    first_head = lane < DIL_HEAD_DIM
    jq = lax.broadcasted_iota(jnp.int32, (sp, 2 * sp), 0)
    jk = lax.broadcasted_iota(jnp.int32, (sp, 2 * sp), 1)
    dist = jq + sp - jk
    valid = (dist >= 0) & (dist <= sp)
    prev_half = jk < sp
    nt = (((1,), (1,)), ((), ()))
    for g, (_, d) in enumerate(DIL_GROUPS):
        dist_f = (dist * d).astype(F32)
        biases = []
        for hh in range(2):
            slope = jnp.exp2(-(8.0 / DIL_HEADS) * (2 * j + hh + 1).astype(F32))
            biases.append(jnp.where(valid, -slope * dist_f, NEG_INF))

        def combo(c, carry, g=g, d=d, biases=biases):
            n = c // d
            r = c % d
            qs = n * (sp * d) + r
            ps = jnp.maximum(n - 1, 0) * (sp * d) + r
            q = q_ref[g, _strided(qs, sp, d), :]
            k = jnp.concatenate([k_ref[g, _strided(ps, sp, d), :], k_ref[g, _strided(qs, sp, d), :]],
                                axis=0).astype(BF16)
            v = jnp.concatenate([v_ref[g, _strided(ps, sp, d), :], v_ref[g, _strided(qs, sp, d), :]],
                                axis=0).astype(BF16)
            no_prev = jnp.where(prev_half & (n == 0), NEG_INF, 0.0)
            res = []
            for hh in range(2):
                keep = first_head if hh == 0 else jnp.logical_not(first_head)
                qh = jnp.where(keep, q, 0.0).astype(BF16)
                s = lax.dot_general(qh, k, nt, preferred_element_type=F32) + biases[hh] + no_prev
                m = jnp.max(s, axis=-1, keepdims=True)
                e = jnp.exp(s - m)
                den = jnp.sum(e, axis=-1, keepdims=True)
                o = jnp.dot(e.astype(BF16), v, preferred_element_type=F32)
                res.append((o, m, den))
            sel = lambda a, b: jnp.where(first_head, a, b)
            os_ref[g, _strided(qs, sp, d), :] = sel(res[0][0], res[1][0])
            ms_ref[g, _strided(qs, sp, d), :] = sel(jnp.broadcast_to(res[0][1], (sp, LANES)),
                                                    jnp.broadcast_to(res[1][1], (sp, LANES)))
            ds_ref[g, _strided(qs, sp, d), :] = sel(jnp.broadcast_to(res[0][2], (sp, LANES)),
                                                    jnp.broadcast_to(res[1][2], (sp, LANES)))
            return carry

        lax.fori_loop(0, SEQ // sp, combo, 0)
    m_all = jnp.maximum(jnp.maximum(ms_ref[0], ms_ref[1]), ms_ref[2])
    num = jnp.zeros((SEQ, LANES), F32)
    den = jnp.zeros((SEQ, LANES), F32)
    for g in range(len(DIL_GROUPS)):
        w = jnp.exp(ms_ref[g] - m_all)
        num = num + os_ref[g] * w
        den = den + ds_ref[g] * w
    o_ref[...] = (num / den).astype(o_ref.dtype)


def _dil_attn(q, kv):
    ng = len(DIL_GROUPS)
    pairs = DIL_HEADS // 2
    blk = lambda off: pl.BlockSpec((ng, SEQ, LANES), lambda b, j, off=off: (0, b, off + j))
    return pl.pallas_call(
        _dil_attn_kernel, grid=(BATCH, pairs),
        in_specs=[blk(0), blk(0), blk(pairs)],
        out_specs=pl.BlockSpec((SEQ, LANES), lambda b, j: (b, j)),
        out_shape=jax.ShapeDtypeStruct((N_PROMPT, DIL_WIDTH), BF16),
        scratch_shapes=[pltpu.VMEM((ng, SEQ, LANES), F32)] * 3,
        compiler_params=_cparams(("parallel", "parallel")), name="dil_attn_prompt",
    )(q, kv, kv)


def _dil_sample_kernel(q_ref, kv_ref, s0_ref, s1_ref, s2_ref, o_ref):
    hi = lax.Precision.HIGHEST
    sp = DIL_SPAN
    w = DIL_WIDTH
    e_mat = (lax.broadcasted_iota(jnp.int32, (w, LANES), 0) // DIL_HEAD_DIM
             == lax.broadcasted_iota(jnp.int32, (w, LANES), 1)).astype(F32)
    et_mat = (lax.broadcasted_iota(jnp.int32, (LANES, w), 1) // DIL_HEAD_DIM
              == lax.broadcasted_iota(jnp.int32, (LANES, w), 0)).astype(F32)
    lane = lax.broadcasted_iota(jnp.int32, (1, LANES), 1)
    slopes = jnp.where(lane < DIL_HEADS, jnp.exp2(-(8.0 / DIL_HEADS) * (lane + 1).astype(F32)), 0.0)
    steps_back = (sp - lax.broadcasted_iota(jnp.int32, (sp, LANES), 0)).astype(F32)
    parts = []
    for g, (s_ref, (_, d)) in enumerate(zip((s0_ref, s1_ref, s2_ref), DIL_GROUPS)):
        q = q_ref[g]
        kv_new = kv_ref[g]
        k_new, v_new = kv_new[:, :w], kv_new[:, w:]
        k_past, v_past = s_ref[:, :w], s_ref[:, w:]
        s_past = jnp.dot(k_past * q, e_mat, precision=hi, preferred_element_type=F32)
        s_past = s_past - slopes * (steps_back * float(d))
        s_new = jnp.dot(k_new * q, e_mat, precision=hi, preferred_element_type=F32)
        m = jnp.maximum(jnp.max(s_past, axis=0, keepdims=True), s_new)
        e_past = jnp.exp(s_past - m)
        e_new = jnp.exp(s_new - m)
        den = jnp.sum(e_past, axis=0, keepdims=True) + e_new
        e_wide = jnp.dot(e_past, et_mat, precision=hi, preferred_element_type=F32)
        o = (jnp.sum(e_wide * v_past, axis=0, keepdims=True)
             + jnp.dot(e_new, et_mat, precision=hi, preferred_element_type=F32) * v_new)
        parts.append((o, m, den))
    m_all = jnp.maximum(jnp.maximum(parts[0][1], parts[1][1]), parts[2][1])
    num = jnp.zeros((1, w), F32)
    den = jnp.zeros((1, LANES), F32)
    for o, m, dn in parts:
        wgt = jnp.exp(m - m_all)
        num = num + o * jnp.dot(wgt, et_mat, precision=hi, preferred_element_type=F32)
        den = den + dn * wgt
    o_ref[...] = (num / jnp.dot(den, et_mat, precision=hi, preferred_element_type=F32)).astype(o_ref.dtype)


def _dil_sample(q_s, kv_s, states, layer):
    ng = len(DIL_GROUPS)
    w = DIL_WIDTH
    st_specs = [pl.BlockSpec((None, None, DIL_SPAN, 2 * w), lambda b: (layer, b, 0, 0)) for _ in range(ng)]
    return pl.pallas_call(
        _dil_sample_kernel, grid=(DEC_BATCH,),
        in_specs=[pl.BlockSpec((ng, None, 1, w), lambda b: (0, b, 0, 0)),
                  pl.BlockSpec((ng, None, 1, 2 * w), lambda b: (0, b, 0, 0))] + st_specs,
        out_specs=pl.BlockSpec((None, 1, w), lambda b: (b, 0, 0)),
        out_shape=jax.ShapeDtypeStruct((DEC_BATCH, 1, w), BF16),
        compiler_params=_cparams(("parallel",)), name="dil_attn_sample",
    )(q_s.reshape(ng, DEC_BATCH, 1, w), kv_s.reshape(ng, DEC_BATCH, 1, 2 * w), *states)


def _rope_tables(pos):
    half = MLA_ROPE // 2
    inv = ROPE_THETA ** (-jnp.arange(half, dtype=F32) / half)
    ang = pos.astype(F32)[:, None] * inv[None, :]
    cos, sin = jnp.cos(ang), jnp.sin(ang)
    n = pos.shape[0]
    scale = (MLA_NOPE + MLA_ROPE) ** -0.5
    zq = jnp.zeros((n, LANES - MLA_NOPE - MLA_ROPE), F32)
    cq = jnp.concatenate([jnp.ones((n, MLA_NOPE), F32), cos, cos, zq], axis=1) * scale
    sq = jnp.concatenate([jnp.zeros((n, MLA_NOPE), F32), -sin, sin, zq], axis=1) * scale
    zk = jnp.zeros((n, LANES - MLA_ROPE), F32)
    ck = jnp.concatenate([cos, cos, zk], axis=1)
    sk = jnp.concatenate([-sin, sin, zk], axis=1)
    return cq, sq, ck, sk


def _mla_weights(w_dq, w_uq, w_dkv, w_uk, w_uv):
    half = MLA_ROPE // 2
    padl = lambda a, wdt: jnp.pad(a, [(0, 0)] * (a.ndim - 1) + [(0, wdt - a.shape[-1])])
    c, r = w_dkv[:, :MLA_KV_LORA], w_dkv[:, MLA_KV_LORA:]
    r_sw = jnp.concatenate([r[:, half:], r[:, :half]], axis=1)
    wa = jnp.concatenate([w_dq, c, padl(r, LANES), padl(r_sw, LANES)], axis=1).astype(BF16)
    uq = w_uq.reshape(MLA_Q_LORA, MLA_HEADS, MLA_NOPE + MLA_ROPE)
    wq1 = padl(uq, LANES).reshape(MLA_Q_LORA, HEAD_PAD).astype(BF16)
    uq_sw = jnp.concatenate([jnp.zeros_like(uq[..., :MLA_NOPE]), uq[..., MLA_NOPE + half:],
                             uq[..., MLA_NOPE:MLA_NOPE + half]], axis=-1)
    wq2 = padl(uq_sw, LANES).reshape(MLA_Q_LORA, HEAD_PAD).astype(BF16)
    wk_top = padl(w_uk, LANES).reshape(MLA_KV_LORA, HEAD_PAD)
    place = jnp.pad(jnp.eye(MLA_ROPE, dtype=F32), ((0, LANES - MLA_ROPE), (MLA_NOPE, LANES - MLA_NOPE - MLA_ROPE)))
    wk = jnp.concatenate([wk_top, jnp.tile(place, (1, MLA_HEADS))], axis=0).astype(BF16)
    wv = w_uv.reshape(MLA_KV_LORA, MLA_HEADS * MLA_VDIM).astype(BF16)
    w_qlat = padl(jnp.transpose(w_uk, (1, 0, 2)), LANES)
    w_qlat = jnp.transpose(w_qlat, (0, 2, 1)).astype(BF16)
    uv = jnp.transpose(w_uv, (1, 0, 2))
    z = jnp.zeros_like(uv)
    even = jnp.concatenate([uv, z], axis=-1)
    odd = jnp.concatenate([z, uv], axis=-1)
    is_even = (jnp.arange(MLA_HEADS) % 2 == 0)[:, None, None]
    w_vpair = jnp.where(is_even, even, odd).astype(BF16)
    return wa, wq1, wq2, wk, wv, w_qlat, w_vpair


def kernel(x_prompt, x_sample, cache_mla_ckv, cache_mla_kpe, page_table, state_dil_kv_w128, state_dil_kv_w512, state_dil_kv_w2048, mla_w_dq, mla_g_q, mla_w_uq, mla_w_dkv, mla_g_kv, mla_w_uk, mla_w_uv, mla_w_o, dil_w_qkv, dil_w_o, norm_mix, norm_ffn, ffn_w_up, ffn_w_down, norm_final):
    dil_states = (state_dil_kv_w128, state_dil_kv_w512, state_dil_kv_w2048)
    ng = len(DIL_GROUPS)
    tm = 512
    xp = x_prompt.reshape(N_PROMPT, D_MODEL)
    xs = x_sample.reshape(DEC_BATCH, D_MODEL)
    tabs_p = _rope_tables(jnp.arange(SEQ))
    tabs_s = _rope_tables(jnp.full((DEC_BATCH,), PAST_LEN))
    gfin = norm_final.reshape(1, D_MODEL)
    ckv_p, kpe_p, ckv_s, kpe_s = [], [], [], []
    dil_p = [[] for _ in DIL_GROUPS]
    dil_s = [[] for _ in DIL_GROUPS]
    for layer in range(DEPTH):
        i = layer // 2
        gmix = norm_mix[layer].reshape(1, D_MODEL)
        if layer % 2 == 0:
            wa, wq1, wq2, wk, wv, w_qlat, w_vpair = _mla_weights(mla_w_dq[i], mla_w_uq[i], mla_w_dkv[i], mla_w_uk[i], mla_w_uv[i])
            gq = mla_g_q[i].reshape(1, MLA_Q_LORA)
            gkv = mla_g_kv[i].reshape(1, MLA_KV_LORA)
            q_p, c_p, k_p, kf_p, v_p = _mla_proj(xp, gmix, wa, gq, wq1, wq2, gkv, tabs_p, wk, wv, tm=tm, with_kv=True)
            q_s, c_s, k_s = _mla_proj(xs, gmix, wa, gq, wq1, wq2, gkv, tabs_s, None, None, tm=DEC_BATCH, with_kv=False)
            o_p = _mla_attn(q_p, kf_p, v_p, tq=256)
            qlat = jnp.transpose(_mla_qlat(q_s, w_qlat), (1, 0, 2))
            qpe = q_s.reshape(DEC_BATCH, MLA_HEADS, LANES)[:, :, MLA_NOPE:MLA_NOPE + MLA_ROPE].astype(F32)
            olat = _mla_sample_attn(page_table, qlat, qpe, c_s.reshape(DEC_BATCH, 1, MLA_KV_LORA),
                                    k_s.reshape(DEC_BATCH, 1, MLA_ROPE), cache_mla_ckv[i], cache_mla_kpe[i])
            o_s = _mla_sample_out(jnp.transpose(olat, (1, 0, 2)), w_vpair)
            wo = mla_w_o[i].astype(BF16)
            ckv_p.append(c_p.reshape(BATCH, SEQ, MLA_KV_LORA))
            kpe_p.append(k_p.reshape(BATCH, SEQ, MLA_ROPE))
            ckv_s.append(c_s.reshape(DEC_BATCH, 1, MLA_KV_LORA))
            kpe_s.append(k_s.reshape(DEC_BATCH, 1, MLA_ROPE))
        else:
            wqkv = dil_w_qkv[i].astype(BF16)
            q_p, kv_p = _dil_proj(xp, gmix, wqkv, tm=256)
            q_s, kv_s = _dil_proj(xs, gmix, wqkv, tm=DEC_BATCH)
            o_p = _dil_attn(q_p, kv_p)
            states = [st.reshape(st.shape[0], DEC_BATCH, DIL_SPAN, d * 2 * DIL_WIDTH)
                      for st, (_, d) in zip(dil_states, DIL_GROUPS)]
            o_s = _dil_sample(q_s, kv_s, states, i).reshape(DEC_BATCH, DIL_WIDTH)
            wo = dil_w_o[i].astype(BF16)
            for gi, (window, _) in enumerate(DIL_GROUPS):
                kvg = kv_p[gi].reshape(BATCH, SEQ, 2, DIL_HEADS, DIL_HEAD_DIM)
                dil_p[gi].append(kvg[:, SEQ - min(window, SEQ):])
                new_row = kv_s[gi].reshape(DEC_BATCH, 1, 2, DIL_HEADS, DIL_HEAD_DIM)
                dil_s[gi].append(jnp.concatenate([dil_states[gi][i][:, 1:], new_row], axis=1))
        final = layer == DEPTH - 1
        g_ffn = norm_ffn[layer].reshape(1, D_MODEL)
        wup = ffn_w_up[layer].astype(BF16)
        wdn = ffn_w_down[layer].astype(BF16)
        xp = _post_ffn(xp, o_p, wo, g_ffn, wup, wdn, gfin, tm=tm, final=final)
        xs = _post_ffn(xs, o_s, wo, g_ffn, wup, wdn, gfin, tm=DEC_BATCH, final=final)
    return (xp.reshape(BATCH, SEQ, D_MODEL), xs.reshape(DEC_BATCH, 1, D_MODEL),
            jnp.stack(ckv_p), jnp.stack(kpe_p), jnp.stack(ckv_s), jnp.stack(kpe_s),
            jnp.stack(dil_p[0]), jnp.stack(dil_p[1]), jnp.stack(dil_p[2]),
            jnp.stack(dil_s[0]), jnp.stack(dil_s[1]), jnp.stack(dil_s[2]))
```

```python
import functools
import math

import jax
import jax.numpy as jnp
from jax import lax
from jax.experimental import pallas as pl
from jax.experimental.pallas import tpu as pltpu

F32 = jnp.float32
BF16 = jnp.bfloat16

D_MODEL = 1024
BATCH = 8
SEQ = 2048
DEPTH = 4
DEC_BATCH = 32
PAST_LEN = 16384
PAGE_SIZE = 128
N_PAGES = PAST_LEN // PAGE_SIZE
MLA_HEADS = 16
MLA_NOPE = 64
MLA_ROPE = 32
MLA_VDIM = 64
MLA_Q_LORA = 384
MLA_KV_LORA = 256
ROPE_THETA = 10000.0
DIL_GROUPS = ((128, 1), (512, 4), (2048, 16))
DIL_SPAN = 128
DIL_HEADS = 8
DIL_HEAD_DIM = 64
DIL_WIDTH = DIL_HEADS * DIL_HEAD_DIM
D_FF = 4 * D_MODEL
EPS = 1e-6
NEG_INF = -1e30
LOG2E = math.log2(math.e)

LANES = 128
N_PROMPT = BATCH * SEQ
HEAD_PAD = MLA_HEADS * LANES
VMEM_LIMIT = 56 * 1024 * 1024
PAGES_PER_STEP = 16
PAGE_SPLIT = 4
DIL_HEAD_BLOCK = 4
NT = (((1,), (1,)), ((), ()))


def _cparams(sem):
    return pltpu.CompilerParams(dimension_semantics=sem, vmem_limit_bytes=VMEM_LIMIT)


def _rms(x, g):
    return x * lax.rsqrt(jnp.mean(x * x, axis=-1, keepdims=True) + EPS) * g


def _const_spec(shape):
    nd = len(shape)
    return pl.BlockSpec(shape, lambda *_: (0,) * nd, pipeline_mode=pl.Buffered(1))


def _any_spec():
    return pl.BlockSpec(memory_space=pl.ANY)


def _mla_proj_kernel(*refs, with_kv):
    if with_kv:
        (x_ref, gmix_ref, wa_ref, gq_ref, wq1_ref, wq2_ref, gkv_ref, cq_ref, sq_ref, ck_ref, sk_ref,
         wk_ref, wv_ref, q_out, ckv_out, kpe_out, k_out, v_out) = refs
    else:
        (x_ref, gmix_ref, wa_ref, gq_ref, wq1_ref, wq2_ref, gkv_ref, cq_ref, sq_ref, ck_ref, sk_ref,
         q_out, ckv_out, kpe_out) = refs
    h = _rms(x_ref[...], gmix_ref[...]).astype(BF16)
    a = jnp.dot(h, wa_ref[...], preferred_element_type=F32)
    c_q = _rms(a[:, :MLA_Q_LORA], gq_ref[...]).astype(BF16)
    o1 = MLA_Q_LORA + MLA_KV_LORA
    c_kv = _rms(a[:, MLA_Q_LORA:o1], gkv_ref[...])
    k_pe = a[:, o1:o1 + LANES] * ck_ref[...] + a[:, o1 + LANES:o1 + 2 * LANES] * sk_ref[...]
    ckv_out[...] = c_kv
    if with_kv:
        kpe_out[...] = k_pe.T[:MLA_ROPE, :]
    else:
        kpe_out[...] = k_pe[:, :MLA_ROPE]
    heads_per_chunk = 4
    cw = heads_per_chunk * LANES
    cq_c = jnp.concatenate([cq_ref[...]] * heads_per_chunk, axis=1)
    sq_c = jnp.concatenate([sq_ref[...]] * heads_per_chunk, axis=1)
    for c in range(MLA_HEADS // heads_per_chunk):
        q1 = jnp.dot(c_q, wq1_ref[:, c * cw:(c + 1) * cw], preferred_element_type=F32)
        q2 = jnp.dot(c_q, wq2_ref[:, c * cw:(c + 1) * cw], preferred_element_type=F32)
        q_out[:, c * cw:(c + 1) * cw] = (q1 * cq_c + q2 * sq_c).astype(q_out.dtype)
    if with_kv:
        c_kv_b = c_kv.astype(BF16)
        lhs = jnp.concatenate([c_kv_b, k_pe.astype(BF16)], axis=1)
        k_out[...] = jnp.dot(lhs, wk_ref[...], preferred_element_type=F32).astype(k_out.dtype)
        v_out[...] = jnp.dot(c_kv_b, wv_ref[...], preferred_element_type=F32).astype(v_out.dtype)


def _mla_proj(x, gmix, wa, gq, wq1, wq2, gkv, tabs, wk, wv, *, tm, with_kv):
    n = x.shape[0]
    nt = n // tm
    tab_blocks = tabs[0].shape[0] // tm
    row = lambda w: pl.BlockSpec((tm, w), lambda i: (i, 0))
    tab = pl.BlockSpec((tm, LANES), lambda i: (i % tab_blocks, 0))
    in_specs = [row(D_MODEL), _const_spec(gmix.shape), _const_spec(wa.shape), _const_spec(gq.shape),
                _const_spec(wq1.shape), _const_spec(wq2.shape), _const_spec(gkv.shape), tab, tab, tab, tab]
    args = [x, gmix, wa, gq, wq1, wq2, gkv, *tabs]
    out_shape = [jax.ShapeDtypeStruct((n, HEAD_PAD), BF16), jax.ShapeDtypeStruct((n, MLA_KV_LORA), F32)]
    out_specs = [row(HEAD_PAD), row(MLA_KV_LORA)]
    if with_kv:
        per_b = SEQ // tm
        in_specs += [_const_spec(wk.shape), _const_spec(wv.shape)]
        args += [wk, wv]
        out_shape += [jax.ShapeDtypeStruct((BATCH, MLA_ROPE, SEQ), F32),
                      jax.ShapeDtypeStruct((n, HEAD_PAD), BF16),
                      jax.ShapeDtypeStruct((n, MLA_HEADS * MLA_VDIM), BF16)]
        out_specs += [pl.BlockSpec((None, MLA_ROPE, tm), lambda i: (i // per_b, 0, i % per_b)),
                      row(HEAD_PAD), row(MLA_HEADS * MLA_VDIM)]
    else:
        out_shape.append(jax.ShapeDtypeStruct((n, MLA_ROPE), F32))
        out_specs.append(row(MLA_ROPE))
    return pl.pallas_call(
        functools.partial(_mla_proj_kernel, with_kv=with_kv),
        grid=(nt,), in_specs=in_specs, out_specs=out_specs, out_shape=out_shape,
        compiler_params=_cparams(("parallel",)), name="mla_proj_kv" if with_kv else "mla_proj_q",
    )(*args)


def _mla_attn_kernel(q_ref, k_ref, v_ref, o_ref, *, tq):
    row = lax.broadcasted_iota(jnp.int32, (tq, tq), 0)
    col = lax.broadcasted_iota(jnp.int32, (tq, tq), 1)
    causal = col <= row
    lane = lax.broadcasted_iota(jnp.int32, (tq, LANES), 1)
    for i in range(SEQ // tq):
        w0 = i * tq
        outs = []
        for hh in range(2):
            hs = slice(hh * LANES, (hh + 1) * LANES)
            q = q_ref[w0:w0 + tq, hs]
            s_d = lax.dot_general(q, k_ref[w0:w0 + tq, hs], NT, preferred_element_type=F32)
            s_d = jnp.where(causal, s_d, NEG_INF)
            m = jnp.max(s_d, axis=-1, keepdims=True)
            if i > 0:
                s_o = lax.dot_general(q, k_ref[0:w0, hs], NT, preferred_element_type=F32)
                m = jnp.maximum(m, jnp.max(s_o, axis=-1, keepdims=True))
            p_d = jnp.exp2(s_d - m)
            l = jnp.sum(p_d, axis=-1, keepdims=True)
            acc = jnp.dot(p_d.astype(BF16), v_ref[w0:w0 + tq, :], preferred_element_type=F32)
            if i > 0:
                p_o = jnp.exp2(s_o - m)
                l = l + jnp.sum(p_o, axis=-1, keepdims=True)
                acc = acc + jnp.dot(p_o.astype(BF16), v_ref[0:w0, :], preferred_element_type=F32)
            outs.append(acc / l)
        o_ref[w0:w0 + tq, :] = jnp.where(lane < MLA_VDIM, outs[0], outs[1]).astype(o_ref.dtype)


def _mla_attn(q, k, v, *, tq):
    pairs = MLA_HEADS // 2
    return pl.pallas_call(
        functools.partial(_mla_attn_kernel, tq=tq),
        grid=(BATCH, pairs),
        in_specs=[pl.BlockSpec((SEQ, 2 * LANES), lambda b, j: (b, j)),
                  pl.BlockSpec((SEQ, 2 * LANES), lambda b, j: (b, j)),
                  pl.BlockSpec((SEQ, LANES), lambda b, j: (b, j))],
        out_specs=pl.BlockSpec((SEQ, LANES), lambda b, j: (b, j)),
        out_shape=jax.ShapeDtypeStruct((N_PROMPT, MLA_HEADS * MLA_VDIM), BF16),
        compiler_params=_cparams(("parallel", "parallel")), name="mla_attn_prompt",
    )(q, k, v)


def _mla_qlat_kernel(q_ref, w_ref, o_ref):
    for h in range(MLA_HEADS):
        o_ref[h] = jnp.dot(q_ref[:, h * LANES:(h + 1) * LANES], w_ref[h], preferred_element_type=F32)


def _mla_qlat(q_s, w_qlat):
    return pl.pallas_call(
        _mla_qlat_kernel,
        out_shape=jax.ShapeDtypeStruct((MLA_HEADS, DEC_BATCH, MLA_KV_LORA), F32),
        compiler_params=pltpu.CompilerParams(vmem_limit_bytes=VMEM_LIMIT), name="mla_qlat",
    )(q_s, w_qlat)


def _mla_sample_attn_kernel(pt_ref, qlat_ref, qpe_ref, cnew_ref, knew_ref, *rest):
    del pt_ref
    npg = PAGES_PER_STEP
    ckv_refs = rest[:npg]
    kpe_refs = rest[npg:2 * npg]
    o_ref, m_s, l_s, acc_s = rest[2 * npg:]
    c = pl.program_id(1)
    qlat = qlat_ref[...]
    qpe = qpe_ref[...]

    @pl.when(c == 0)
    def _():
        cnew = cnew_ref[...]
        s_new = (jnp.sum(qlat * cnew, axis=-1, keepdims=True)
                 + jnp.sum(qpe * knew_ref[...], axis=-1, keepdims=True))
        m_s[...] = s_new
        l_s[...] = jnp.ones_like(s_new)
        acc_s[...] = jnp.broadcast_to(cnew, acc_s.shape)

    qlat_b = qlat.astype(BF16)
    qpe_b = qpe.astype(BF16)
    per = npg // PAGE_SPLIT
    chains = []
    for j in range(PAGE_SPLIT):
        ckv = jnp.concatenate([r[...] for r in ckv_refs[j * per:(j + 1) * per]], axis=0).astype(BF16)
        kpe_t = jnp.concatenate([r[...] for r in kpe_refs[j * per:(j + 1) * per]], axis=1).astype(BF16)
        s = (lax.dot_general(qlat_b, ckv, NT, preferred_element_type=F32)
             + jnp.dot(qpe_b, kpe_t, preferred_element_type=F32))
        m_j = jnp.max(s, axis=-1, keepdims=True)
        p = jnp.exp2(s - m_j)
        l_j = jnp.sum(p, axis=-1, keepdims=True)
        a_j = jnp.dot(p.astype(BF16), ckv, preferred_element_type=F32)
        chains.append((m_j, l_j, a_j))
    m_old = m_s[...]
    m_new = m_old
    for m_j, _, _ in chains:
        m_new = jnp.maximum(m_new, m_j)
    alpha = jnp.exp2(m_old - m_new)
    l = alpha * l_s[...]
    acc = alpha * acc_s[...]
    for m_j, l_j, a_j in chains:
        w_j = jnp.exp2(m_j - m_new)
        l = l + w_j * l_j
        acc = acc + w_j * a_j
    m_s[...] = m_new
    l_s[...] = l
    acc_s[...] = acc

    @pl.when(c == pl.num_programs(1) - 1)
    def _():
        o_ref[...] = acc / l


def _mla_sample_attn(page_table, qlat, qpe, cnew, knew, ckv_pool, kpe_pool_t, layer):
    npg = PAGES_PER_STEP
    nchunk = N_PAGES // npg
    per_b = lambda w, r: pl.BlockSpec((None, r, w), lambda b, c, pt: (b, 0, 0))
    page = lambda r, w, k: pl.BlockSpec((None, None, r, w), lambda b, c, pt, k=k: (layer, pt[b, c * npg + k], 0, 0))
    in_specs = ([per_b(MLA_KV_LORA, MLA_HEADS), per_b(MLA_ROPE, MLA_HEADS), per_b(MLA_KV_LORA, 1), per_b(MLA_ROPE, 1)]
                + [page(PAGE_SIZE, MLA_KV_LORA, k) for k in range(npg)]
                + [page(MLA_ROPE, PAGE_SIZE, k) for k in range(npg)])
    grid_spec = pltpu.PrefetchScalarGridSpec(
        num_scalar_prefetch=1, grid=(DEC_BATCH, nchunk), in_specs=in_specs,
        out_specs=pl.BlockSpec((None, MLA_HEADS, MLA_KV_LORA), lambda b, c, pt: (b, 0, 0)),
        scratch_shapes=[pltpu.VMEM((MLA_HEADS, 1), F32), pltpu.VMEM((MLA_HEADS, 1), F32),
                        pltpu.VMEM((MLA_HEADS, MLA_KV_LORA), F32)])
    return pl.pallas_call(
        _mla_sample_attn_kernel, grid_spec=grid_spec,
        out_shape=jax.ShapeDtypeStruct((DEC_BATCH, MLA_HEADS, MLA_KV_LORA), F32),
        compiler_params=_cparams(("parallel", "arbitrary")), name="mla_attn_sample",
    )(page_table, qlat, qpe, cnew, knew, *([ckv_pool] * npg), *([kpe_pool_t] * npg))


def _mla_sample_out_kernel(olat_ref, w_ref, o_ref):
    for j in range(MLA_HEADS // 2):
        acc = None
        for h in (2 * j, 2 * j + 1):
            t = jnp.dot(olat_ref[h].astype(BF16), w_ref[h], preferred_element_type=F32)
            acc = t if acc is None else acc + t
        o_ref[:, j * LANES:(j + 1) * LANES] = acc.astype(o_ref.dtype)


def _mla_sample_out(olat_t, w_vpair):
    return pl.pallas_call(
        _mla_sample_out_kernel,
        out_shape=jax.ShapeDtypeStruct((DEC_BATCH, MLA_HEADS * MLA_VDIM), BF16),
        compiler_params=pltpu.CompilerParams(vmem_limit_bytes=VMEM_LIMIT), name="mla_sample_out",
    )(olat_t, w_vpair)


def _post_ffn_kernel(*refs, final, ff_chunk):
    if final:
        x_ref, o_ref, wo_ref, g_ref, wup_ref, wdn_ref, gfin_ref, out_ref = refs
    else:
        x_ref, o_ref, wo_ref, g_ref, wup_ref, wdn_ref, out_ref = refs
    x1 = x_ref[...] + jnp.dot(o_ref[...], wo_ref[...], preferred_element_type=F32)
    hn = _rms(x1, g_ref[...]).astype(BF16)
    acc = x1
    for c in range(D_FF // ff_chunk):
        u = jnp.dot(hn, wup_ref[:, c * ff_chunk:(c + 1) * ff_chunk], preferred_element_type=F32)
        u = jnp.square(jnp.maximum(u, 0.0)).astype(BF16)
        acc = acc + jnp.dot(u, wdn_ref[c * ff_chunk:(c + 1) * ff_chunk, :], preferred_element_type=F32)
    if final:
        acc = _rms(acc, gfin_ref[...])
    out_ref[...] = acc


def _post_ffn(x, o, wo, g, wup, wdn, gfin, *, tm, final):
    n = x.shape[0]
    ko = o.shape[1]
    row = lambda w: pl.BlockSpec((tm, w), lambda i: (i, 0))
    in_specs = [row(D_MODEL), row(ko), _const_spec(wo.shape), _const_spec(g.shape),
                _const_spec(wup.shape), _const_spec(wdn.shape)]
    args = [x, o, wo, g, wup, wdn]
    if final:
        in_specs.append(_const_spec(gfin.shape))
        args.append(gfin)
    return pl.pallas_call(
        functools.partial(_post_ffn_kernel, final=final, ff_chunk=1024),
        grid=(n // tm,), in_specs=in_specs, out_specs=row(D_MODEL),
        out_shape=jax.ShapeDtypeStruct((n, D_MODEL), F32),
        compiler_params=_cparams(("parallel",)), name="post_ffn_final" if final else "post_ffn",
    )(*args)


def _dil_proj_prompt_kernel(*refs, aliased):
    x_ref, g_ref, w_ref = refs[:3]
    q_out, kv_out, b128_out, b512_out, b2048_out = refs[-5:]
    del aliased
    h = _rms(x_ref[...], g_ref[...]).astype(BF16)
    gw = 3 * DIL_WIDTH
    tm = x_ref.shape[0]
    for g, buf_out in enumerate((b128_out, b512_out, b2048_out)):
        r = jnp.dot(h, w_ref[:, g * gw:(g + 1) * gw], preferred_element_type=F32)
        q_out[g] = r[:, :DIL_WIDTH] * (DIL_HEAD_DIM ** -0.5)
        kv = r[:, DIL_WIDTH:]
        kv_out[g] = kv
        kv_t = kv.T
        wb = buf_out.shape[-1]
        buf_out[...] = kv_t[:, tm - wb:] if wb < tm else kv_t


def _dil_proj_prompt(x, g, w, layer, prev, *, tm):
    n = x.shape[0]
    ng = len(DIL_GROUPS)
    per_b = SEQ // tm
    kvw = 2 * DIL_WIDTH
    buf_shapes, buf_specs = [], []
    for window, _ in DIL_GROUPS:
        wb = min(window, tm)
        nblk = window // wb
        first = per_b - nblk
        buf_shapes.append(jax.ShapeDtypeStruct((DEPTH // 2, BATCH, kvw, window), F32))
        buf_specs.append(pl.BlockSpec(
            (None, None, kvw, wb),
            lambda i, first=first: (layer, i // per_b, 0, jnp.maximum(i % per_b - first, 0))))
    in_specs = [pl.BlockSpec((tm, D_MODEL), lambda i: (i, 0)), _const_spec(g.shape), _const_spec(w.shape)]
    args = [x, g, w]
    aliases = {}
    if prev is not None:
        in_specs += [_any_spec()] * ng
        args += list(prev)
        aliases = {3 + k: 2 + k for k in range(ng)}
    return pl.pallas_call(
        functools.partial(_dil_proj_prompt_kernel, aliased=prev is not None), grid=(n // tm,),
        in_specs=in_specs,
        out_specs=[pl.BlockSpec((ng, tm, DIL_WIDTH), lambda i: (0, i, 0)),
                   pl.BlockSpec((ng, tm, kvw), lambda i: (0, i, 0))] + buf_specs,
        out_shape=[jax.ShapeDtypeStruct((ng, n, DIL_WIDTH), F32), jax.ShapeDtypeStruct((ng, n, kvw), F32)] + buf_shapes,
        input_output_aliases=aliases,
        compiler_params=_cparams(("arbitrary",)), name="dil_proj_prompt",
    )(*args)


def _dil_proj_sample_kernel(x_ref, g_ref, w_ref, q_out, kv_out, kvt_out):
    h = _rms(x_ref[...], g_ref[...]).astype(BF16)
    gw = 3 * DIL_WIDTH
    pad = jnp.zeros((LANES - DEC_BATCH, 2 * DIL_WIDTH), F32)
    for g in range(len(DIL_GROUPS)):
        r = jnp.dot(h, w_ref[:, g * gw:(g + 1) * gw], preferred_element_type=F32)
        q_out[g] = r[:, :DIL_WIDTH] * (DIL_HEAD_DIM ** -0.5)
        kv = r[:, DIL_WIDTH:]
        kv_out[g] = kv
        kvt_out[g] = jnp.concatenate([kv, pad], axis=0).T


def _dil_proj_sample(x, g, w):
    ng = len(DIL_GROUPS)
    return pl.pallas_call(
        _dil_proj_sample_kernel,
        out_shape=[jax.ShapeDtypeStruct((ng, DEC_BATCH, DIL_WIDTH), F32),
                   jax.ShapeDtypeStruct((ng, DEC_BATCH, 2 * DIL_WIDTH), F32),
                   jax.ShapeDtypeStruct((ng, 2 * DIL_WIDTH, LANES), F32)],
        compiler_params=pltpu.CompilerParams(vmem_limit_bytes=VMEM_LIMIT), name="dil_proj_sample",
    )(x, g, w)


def _strided(start, size, stride):
    return pl.ds(start, size) if stride == 1 else pl.ds(start, size, stride=stride)


def _dil_attn_kernel(q_ref, k_ref, v_ref, o_ref, os_ref, ms_ref, ds_ref):
    j = pl.program_id(1)
    sp = DIL_SPAN
    lane = lax.broadcasted_iota(jnp.int32, (sp, LANES), 1)
    first_head = lane < DIL_HEAD_DIM
    jq = lax.broadcasted_iota(jnp.int32, (sp, 2 * sp), 0)
    jk = lax.broadcasted_iota(jnp.int32, (sp, 2 * sp), 1)
    dist = jq + sp - jk
    valid = (dist >= 0) & (dist <= sp)
    slopes = [jnp.exp2(-(8.0 / DIL_HEADS) * (2 * j + hh + 1).astype(F32)) for hh in range(2)]
    sel = lambda a, b: jnp.where(first_head, a, b)
    wide = lambda a: jnp.broadcast_to(a, (sp, LANES))
    for g, (_, d) in enumerate(DIL_GROUPS):
        dist_f = (dist * d).astype(F32)
        bias_full = [jnp.where(valid, -slopes[hh] * dist_f, NEG_INF) for hh in range(2)]
        bias_cur = [b[:, sp:] for b in bias_full]
        for n in range(SEQ // (sp * d)):
            for r in range(d):
                qs = n * sp * d + r
                rows = _strided(qs, sp, d)
                q = q_ref[g, rows, :]
                k = k_ref[g, rows, :]
                v = v_ref[g, rows, :]
                if n > 0:
                    prev = _strided(qs - sp * d, sp, d)
                    k = jnp.concatenate([k_ref[g, prev, :], k], axis=0)
                    v = jnp.concatenate([v_ref[g, prev, :], v], axis=0)
                k = k.astype(BF16)
                v = v.astype(BF16)
                res = []
                for hh in range(2):
                    keep = first_head if hh == 0 else jnp.logical_not(first_head)
                    qh = jnp.where(keep, q, 0.0).astype(BF16)
                    s = lax.dot_general(qh, k, NT, preferred_element_type=F32)
                    s = s + (bias_full[hh] if n > 0 else bias_cur[hh])
                    m = jnp.max(s, axis=-1, keepdims=True)
                    e = jnp.exp(s - m)
                    den = jnp.sum(e, axis=-1, keepdims=True)
                    o = jnp.dot(e.astype(BF16), v, preferred_element_type=F32)
                    res.append((o, m, den))
                os_ref[g, rows, :] = sel(res[0][0], res[1][0])
                ms_ref[g, rows, :] = sel(wide(res[0][1]), wide(res[1][1]))
                ds_ref[g, rows, :] = sel(wide(res[0][2]), wide(res[1][2]))
    m_all = jnp.maximum(jnp.maximum(ms_ref[0], ms_ref[1]), ms_ref[2])
    num = jnp.zeros((SEQ, LANES), F32)
    den = jnp.zeros((SEQ, LANES), F32)
    for g in range(len(DIL_GROUPS)):
        w = jnp.exp(ms_ref[g] - m_all)
        num = num + os_ref[g] * w
        den = den + ds_ref[g] * w
    o_ref[...] = (num / den).astype(o_ref.dtype)


def _dil_attn(q, kv):
    ng = len(DIL_GROUPS)
    pairs = DIL_HEADS // 2
    blk = lambda off: pl.BlockSpec((ng, SEQ, LANES), lambda b, j, off=off: (0, b, off + j))
    return pl.pallas_call(
        _dil_attn_kernel, grid=(BATCH, pairs),
        in_specs=[blk(0), blk(0), blk(pairs)],
        out_specs=pl.BlockSpec((SEQ, LANES), lambda b, j: (b, j)),
        out_shape=jax.ShapeDtypeStruct((N_PROMPT, DIL_WIDTH), BF16),
        scratch_shapes=[pltpu.VMEM((ng, SEQ, LANES), F32)] * 3,
        compiler_params=_cparams(("parallel", "parallel")), name="dil_attn_prompt",
    )(q, kv, kv)


def _dil_sample_kernel(*refs, aliased):
    q_ref, kr_ref, vr_ref, kc_ref, vc_ref, s0_ref, s1_ref, s2_ref = refs[:8]
    o_ref, n0_ref, n1_ref, n2_ref = refs[-4:]
    del aliased
    b = pl.program_id(0)
    hb = pl.program_id(1)
    rw = DIL_HEAD_BLOCK * DIL_HEAD_DIM
    rowi = lax.broadcasted_iota(jnp.int32, (8, rw), 0)
    coli = lax.broadcasted_iota(jnp.int32, (8, rw), 1)
    bd = (coli // DIL_HEAD_DIM) == rowi
    head = hb * DIL_HEAD_BLOCK + lax.broadcasted_iota(jnp.int32, (8, 1), 0)
    slope = jnp.exp2(-(8.0 / DIL_HEADS) * (head + 1).astype(F32))
    is_b = lax.broadcasted_iota(jnp.int32, (rw, LANES), 1) == b
    parts = []
    for g, (s_ref, n_ref, (window, d)) in enumerate(zip((s0_ref, s1_ref, s2_ref), (n0_ref, n1_ref, n2_ref), DIL_GROUPS)):
        q_bd = jnp.where(bd, q_ref[g], 0.0)
        k_old = s_ref[0].reshape(rw, window)
        v_old = s_ref[1].reshape(rw, window)
        s = jnp.dot(q_bd.astype(BF16), k_old.astype(BF16), preferred_element_type=F32)
        pos = lax.broadcasted_iota(jnp.int32, (8, window), 1)
        back = (window - pos).astype(F32)
        s = jnp.where((pos & (d - 1)) == 0, s - slope * back, NEG_INF)
        s_new = jnp.sum(q_bd * kr_ref[g], axis=1, keepdims=True)
        m = jnp.maximum(jnp.max(s, axis=1, keepdims=True), s_new)
        e = jnp.exp(s - m)
        e_new = jnp.exp(s_new - m)
        den = jnp.sum(e, axis=1, keepdims=True) + e_new
        o = (lax.dot_general(e.astype(BF16), v_old.astype(BF16), NT, preferred_element_type=F32)
             + e_new * jnp.where(bd, vr_ref[g], 0.0))
        parts.append((o, m, den))
        last = lax.broadcasted_iota(jnp.int32, (rw, window), 1) == window - 1
        for t, (old, c_ref) in enumerate(((k_old, kc_ref), (v_old, vc_ref))):
            col = jnp.sum(jnp.where(is_b, c_ref[g], 0.0), axis=1, keepdims=True)
            new = jnp.where(last, col, pltpu.roll(old, window - 1, 1))
            n_ref[t] = new.reshape(DIL_HEAD_BLOCK, DIL_HEAD_DIM, window)
    m_all = jnp.maximum(jnp.maximum(parts[0][1], parts[1][1]), parts[2][1])
    num = jnp.zeros((8, rw), F32)
    den = jnp.zeros((8, 1), F32)
    for o, m, dn in parts:
        wgt = jnp.exp(m - m_all)
        num = num + o * wgt
        den = den + dn * wgt
    o_ref[...] = jnp.sum(jnp.where(bd, num / den, 0.0), axis=0, keepdims=True).astype(o_ref.dtype)


def _dil_sample(q_s, kv_s, kvt_s, states_t, layer, prev):
    ng = len(DIL_GROUPS)
    w = DIL_WIDTH
    hbk = DIL_HEAD_BLOCK
    rw = hbk * DIL_HEAD_DIM
    nhb = DIL_HEADS // hbk
    row = lambda off: pl.BlockSpec((ng, None, 1, rw), lambda b, hb, off=off: (0, b, 0, off + hb))
    colspec = lambda off: pl.BlockSpec((ng, rw, LANES), lambda b, hb, off=off: (0, off + hb, 0))
    st_spec = lambda window: pl.BlockSpec((None, None, 2, hbk, DIL_HEAD_DIM, window),
                                          lambda b, hb: (layer, b, 0, hb, 0, 0))
    st_specs = [st_spec(window) for window, _ in DIL_GROUPS]
    in_specs = [row(0), row(0), row(nhb), colspec(0), colspec(nhb)] + st_specs
    args = [q_s.reshape(ng, DEC_BATCH, 1, w), kv_s.reshape(ng, DEC_BATCH, 1, 2 * w),
            kv_s.reshape(ng, DEC_BATCH, 1, 2 * w), kvt_s, kvt_s, *states_t]
    aliases = {}
    if prev is not None:
        in_specs += [_any_spec()] * ng
        args += list(prev)
        aliases = {len(args) - ng + k: 1 + k for k in range(ng)}
    return pl.pallas_call(
        functools.partial(_dil_sample_kernel, aliased=prev is not None), grid=(DEC_BATCH, nhb),
        in_specs=in_specs,
        out_specs=[pl.BlockSpec((None, 1, rw), lambda b, hb: (b, 0, hb))] + st_specs,
        out_shape=[jax.ShapeDtypeStruct((DEC_BATCH, 1, w), BF16)]
                  + [jax.ShapeDtypeStruct(st.shape, F32) for st in states_t],
        input_output_aliases=aliases,
        compiler_params=_cparams(("parallel", "parallel")), name="dil_attn_sample",
    )(*args)


def _rope_tables(pos):
    half = MLA_ROPE // 2
    inv = ROPE_THETA ** (-jnp.arange(half, dtype=F32) / half)
    ang = pos.astype(F32)[:, None] * inv[None, :]
    cos, sin = jnp.cos(ang), jnp.sin(ang)
    n = pos.shape[0]
    scale = (MLA_NOPE + MLA_ROPE) ** -0.5 * LOG2E
    zq = jnp.zeros((n, LANES - MLA_NOPE - MLA_ROPE), F32)
    cq = jnp.concatenate([jnp.ones((n, MLA_NOPE), F32), cos, cos, zq], axis=1) * scale
    sq = jnp.concatenate([jnp.zeros((n, MLA_NOPE), F32), -sin, sin, zq], axis=1) * scale
    zk = jnp.zeros((n, LANES - MLA_ROPE), F32)
    ck = jnp.concatenate([cos, cos, zk], axis=1)
    sk = jnp.concatenate([-sin, sin, zk], axis=1)
    return cq, sq, ck, sk


def _mla_weights(w_dq, w_uq, w_dkv, w_uk, w_uv):
    half = MLA_ROPE // 2
    padl = lambda a, wdt: jnp.pad(a, [(0, 0)] * (a.ndim - 1) + [(0, wdt - a.shape[-1])])
    c, r = w_dkv[:, :MLA_KV_LORA], w_dkv[:, MLA_KV_LORA:]
    r_sw = jnp.concatenate([r[:, half:], r[:, :half]], axis=1)
    wa = jnp.concatenate([w_dq, c, padl(r, LANES), padl(r_sw, LANES)], axis=1).astype(BF16)
    uq = w_uq.reshape(MLA_Q_LORA, MLA_HEADS, MLA_NOPE + MLA_ROPE)
    wq1 = padl(uq, LANES).reshape(MLA_Q_LORA, HEAD_PAD).astype(BF16)
    uq_sw = jnp.concatenate([jnp.zeros_like(uq[..., :MLA_NOPE]), uq[..., MLA_NOPE + half:],
                             uq[..., MLA_NOPE:MLA_NOPE + half]], axis=-1)
    wq2 = padl(uq_sw, LANES).reshape(MLA_Q_LORA, HEAD_PAD).astype(BF16)
    wk_top = padl(w_uk, LANES).reshape(MLA_KV_LORA, HEAD_PAD)
    place = jnp.pad(jnp.eye(MLA_ROPE, dtype=F32), ((0, LANES - MLA_ROPE), (MLA_NOPE, LANES - MLA_NOPE - MLA_ROPE)))
    wk = jnp.concatenate([wk_top, jnp.tile(place, (1, MLA_HEADS))], axis=0).astype(BF16)
    wv = w_uv.reshape(MLA_KV_LORA, MLA_HEADS * MLA_VDIM).astype(BF16)
    w_qlat = padl(jnp.transpose(w_uk, (1, 0, 2)), LANES)
    w_qlat = jnp.transpose(w_qlat, (0, 2, 1)).astype(BF16)
    uv = jnp.transpose(w_uv, (1, 0, 2))
    z = jnp.zeros_like(uv)
    even = jnp.concatenate([uv, z], axis=-1)
    odd = jnp.concatenate([z, uv], axis=-1)
    is_even = (jnp.arange(MLA_HEADS) % 2 == 0)[:, None, None]
    w_vpair = jnp.where(is_even, even, odd).astype(BF16)
    return wa, wq1, wq2, wk, wv, w_qlat, w_vpair


def kernel(x_prompt, x_sample, cache_mla_ckv, cache_mla_kpe, page_table, state_dil_kv_w128, state_dil_kv_w512, state_dil_kv_w2048, mla_w_dq, mla_g_q, mla_w_uq, mla_w_dkv, mla_g_kv, mla_w_uk, mla_w_uv, mla_w_o, dil_w_qkv, dil_w_o, norm_mix, norm_ffn, ffn_w_up, ffn_w_down, norm_final):
    tm = 512
    xp = x_prompt.reshape(N_PROMPT, D_MODEL)
    xs = x_sample.reshape(DEC_BATCH, D_MODEL)
    tabs_p = _rope_tables(jnp.arange(SEQ))
    tabs_s = _rope_tables(jnp.full((DEC_BATCH,), PAST_LEN))
    gfin = norm_final.reshape(1, D_MODEL)
    kpe_pool_t = jnp.transpose(cache_mla_kpe, (0, 1, 3, 2))
    states_t = [jnp.transpose(st, (0, 1, 3, 4, 5, 2))
                for st in (state_dil_kv_w128, state_dil_kv_w512, state_dil_kv_w2048)]
    ckv_p, kpe_p, ckv_s, kpe_s = [], [], [], []
    dil_p_bufs = None
    dil_s_bufs = None
    for layer in range(DEPTH):
        i = layer // 2
        gmix = norm_mix[layer].reshape(1, D_MODEL)
        if layer % 2 == 0:
            wa, wq1, wq2, wk, wv, w_qlat, w_vpair = _mla_weights(mla_w_dq[i], mla_w_uq[i], mla_w_dkv[i], mla_w_uk[i], mla_w_uv[i])
            gq = mla_g_q[i].reshape(1, MLA_Q_LORA)
            gkv = mla_g_kv[i].reshape(1, MLA_KV_LORA)
            q_p, c_p, k_p, kf_p, v_p = _mla_proj(xp, gmix, wa, gq, wq1, wq2, gkv, tabs_p, wk, wv, tm=tm, with_kv=True)
            q_s, c_s, k_s = _mla_proj(xs, gmix, wa, gq, wq1, wq2, gkv, tabs_s, None, None, tm=DEC_BATCH, with_kv=False)
            o_p = _mla_attn(q_p, kf_p, v_p, tq=256)
            qlat = jnp.transpose(_mla_qlat(q_s, w_qlat), (1, 0, 2))
            qpe = q_s.reshape(DEC_BATCH, MLA_HEADS, LANES)[:, :, MLA_NOPE:MLA_NOPE + MLA_ROPE].astype(F32)
            olat = _mla_sample_attn(page_table, qlat, qpe, c_s.reshape(DEC_BATCH, 1, MLA_KV_LORA),
                                    k_s.reshape(DEC_BATCH, 1, MLA_ROPE), cache_mla_ckv, kpe_pool_t, i)
            o_s = _mla_sample_out(jnp.transpose(olat, (1, 0, 2)), w_vpair)
            wo = mla_w_o[i].astype(BF16)
            ckv_p.append(c_p.reshape(BATCH, SEQ, MLA_KV_LORA))
            kpe_p.append(k_p)
            ckv_s.append(c_s.reshape(DEC_BATCH, 1, MLA_KV_LORA))
            kpe_s.append(k_s.reshape(DEC_BATCH, 1, MLA_ROPE))
        else:
            wqkv = dil_w_qkv[i].astype(BF16)
            q_p, kv_p, *dil_p_bufs = _dil_proj_prompt(xp, gmix, wqkv, i, dil_p_bufs, tm=256)
            q_s, kv_s, kvt_s = _dil_proj_sample(xs, gmix, wqkv)
            o_p = _dil_attn(q_p, kv_p)
            o_s, *dil_s_bufs = _dil_sample(q_s, kv_s, kvt_s, states_t, i, dil_s_bufs)
            o_s = o_s.reshape(DEC_BATCH, DIL_WIDTH)
            wo = dil_w_o[i].astype(BF16)
        final = layer == DEPTH - 1
        g_ffn = norm_ffn[layer].reshape(1, D_MODEL)
        wup = ffn_w_up[layer].astype(BF16)
        wdn = ffn_w_down[layer].astype(BF16)
        xp = _post_ffn(xp, o_p, wo, g_ffn, wup, wdn, gfin, tm=tm, final=final)
        xs = _post_ffn(xs, o_s, wo, g_ffn, wup, wdn, gfin, tm=DEC_BATCH, final=final)
    nl = DEPTH // 2
    to_window = lambda a, nb: jnp.transpose(
        a.reshape(nl, nb, 2, DIL_HEADS, DIL_HEAD_DIM, a.shape[-1]), (0, 1, 5, 2, 3, 4))
    return (xp.reshape(BATCH, SEQ, D_MODEL), xs.reshape(DEC_BATCH, 1, D_MODEL),
            jnp.stack(ckv_p), jnp.transpose(jnp.stack(kpe_p), (0, 1, 3, 2)), jnp.stack(ckv_s), jnp.stack(kpe_s),
            *[to_window(a, BATCH) for a in dil_p_bufs],
            *[to_window(a, DEC_BATCH) for a in dil_s_bufs])
```

```python
import functools
import math

import jax
import jax.numpy as jnp
from jax import lax
from jax.experimental import pallas as pl
from jax.experimental.pallas import tpu as pltpu

F32 = jnp.float32
BF16 = jnp.bfloat16

D_MODEL = 1024
BATCH = 8
SEQ = 2048
DEPTH = 4
DEC_BATCH = 32
PAST_LEN = 16384
PAGE_SIZE = 128
N_PAGES = PAST_LEN // PAGE_SIZE
MLA_HEADS = 16
MLA_NOPE = 64
MLA_ROPE = 32
MLA_VDIM = 64
MLA_Q_LORA = 384
MLA_KV_LORA = 256
ROPE_THETA = 10000.0
DIL_GROUPS = ((128, 1), (512, 4), (2048, 16))
DIL_SPAN = 128
DIL_HEADS = 8
DIL_HEAD_DIM = 64
DIL_WIDTH = DIL_HEADS * DIL_HEAD_DIM
D_FF = 4 * D_MODEL
EPS = 1e-6
NEG_INF = -1e30
LOG2E = math.log2(math.e)

LANES = 128
N_PROMPT = BATCH * SEQ
HEAD_PAD = MLA_HEADS * LANES
VMEM_LIMIT = 56 * 1024 * 1024
PAGES_PER_STEP = 16
PAGE_SPLIT = 4
DIL_HEAD_BLOCK = 4
NT = (((1,), (1,)), ((), ()))
TN = (((0,), (0,)), ((), ()))


def _cparams(sem):
    return pltpu.CompilerParams(dimension_semantics=sem, vmem_limit_bytes=VMEM_LIMIT)


def _rms(x, g):
    return x * lax.rsqrt(jnp.mean(x * x, axis=-1, keepdims=True) + EPS) * g


def _const_spec(shape):
    nd = len(shape)
    return pl.BlockSpec(shape, lambda *_: (0,) * nd, pipeline_mode=pl.Buffered(1))


def _any_spec():
    return pl.BlockSpec(memory_space=pl.ANY)


def _mla_proj_kernel(*refs, with_kv):
    if with_kv:
        (x_ref, gmix_ref, wa_ref, gq_ref, wq1_ref, wq2_ref, gkv_ref, cq_ref, sq_ref, ck_ref, sk_ref,
         wk_ref, wv_ref, q_out, ckv_out, kpe_out, k_out, v_out) = refs
    else:
        (x_ref, gmix_ref, wa_ref, gq_ref, wq1_ref, wq2_ref, gkv_ref, cq_ref, sq_ref, ck_ref, sk_ref,
         q_out, ckv_out, kpe_out) = refs
    h = _rms(x_ref[...], gmix_ref[...]).astype(BF16)
    a = jnp.dot(h, wa_ref[...], preferred_element_type=F32)
    c_q = _rms(a[:, :MLA_Q_LORA], gq_ref[...]).astype(BF16)
    o1 = MLA_Q_LORA + MLA_KV_LORA
    c_kv = _rms(a[:, MLA_Q_LORA:o1], gkv_ref[...])
    k_pe = a[:, o1:o1 + LANES] * ck_ref[...] + a[:, o1 + LANES:o1 + 2 * LANES] * sk_ref[...]
    ckv_out[...] = c_kv
    if with_kv:
        kpe_out[...] = k_pe.T[:MLA_ROPE, :]
    else:
        kpe_out[...] = k_pe[:, :MLA_ROPE]
    heads_per_chunk = 4
    cw = heads_per_chunk * LANES
    cq_c = jnp.concatenate([cq_ref[...]] * heads_per_chunk, axis=1)
    sq_c = jnp.concatenate([sq_ref[...]] * heads_per_chunk, axis=1)
    for c in range(MLA_HEADS // heads_per_chunk):
        q1 = jnp.dot(c_q, wq1_ref[:, c * cw:(c + 1) * cw], preferred_element_type=F32)
        q2 = jnp.dot(c_q, wq2_ref[:, c * cw:(c + 1) * cw], preferred_element_type=F32)
        q_out[:, c * cw:(c + 1) * cw] = (q1 * cq_c + q2 * sq_c).astype(q_out.dtype)
    if with_kv:
        c_kv_b = c_kv.astype(BF16)
        lhs = jnp.concatenate([c_kv_b, k_pe.astype(BF16)], axis=1)
        k_out[...] = jnp.dot(lhs, wk_ref[...], preferred_element_type=F32).astype(k_out.dtype)
        v_out[...] = jnp.dot(c_kv_b, wv_ref[...], preferred_element_type=F32).astype(v_out.dtype)


def _mla_proj(x, gmix, wa, gq, wq1, wq2, gkv, tabs, wk, wv, *, tm, with_kv):
    n = x.shape[0]
    nt = n // tm
    tab_blocks = tabs[0].shape[0] // tm
    row = lambda w: pl.BlockSpec((tm, w), lambda i: (i, 0))
    tab = pl.BlockSpec((tm, LANES), lambda i: (i % tab_blocks, 0))
    in_specs = [row(D_MODEL), _const_spec(gmix.shape), _const_spec(wa.shape), _const_spec(gq.shape),
                _const_spec(wq1.shape), _const_spec(wq2.shape), _const_spec(gkv.shape), tab, tab, tab, tab]
    args = [x, gmix, wa, gq, wq1, wq2, gkv, *tabs]
    out_shape = [jax.ShapeDtypeStruct((n, HEAD_PAD), BF16), jax.ShapeDtypeStruct((n, MLA_KV_LORA), F32)]
    out_specs = [row(HEAD_PAD), row(MLA_KV_LORA)]
    if with_kv:
        per_b = SEQ // tm
        in_specs += [_const_spec(wk.shape), _const_spec(wv.shape)]
        args += [wk, wv]
        out_shape += [jax.ShapeDtypeStruct((BATCH, MLA_ROPE, SEQ), F32),
                      jax.ShapeDtypeStruct((n, HEAD_PAD), BF16),
                      jax.ShapeDtypeStruct((n, MLA_HEADS * MLA_VDIM), BF16)]
        out_specs += [pl.BlockSpec((None, MLA_ROPE, tm), lambda i: (i // per_b, 0, i % per_b)),
                      row(HEAD_PAD), row(MLA_HEADS * MLA_VDIM)]
    else:
        out_shape.append(jax.ShapeDtypeStruct((n, MLA_ROPE), F32))
        out_specs.append(row(MLA_ROPE))
    return pl.pallas_call(
        functools.partial(_mla_proj_kernel, with_kv=with_kv),
        grid=(nt,), in_specs=in_specs, out_specs=out_specs, out_shape=out_shape,
        compiler_params=_cparams(("parallel",)), name="mla_proj_kv" if with_kv else "mla_proj_q",
    )(*args)


def _mla_attn_kernel(q_ref, k_ref, v_ref, o_ref, *, tq):
    row = lax.broadcasted_iota(jnp.int32, (tq, tq), 0)
    col = lax.broadcasted_iota(jnp.int32, (tq, tq), 1)
    causal = col <= row
    lane = lax.broadcasted_iota(jnp.int32, (tq, LANES), 1)
    for i in range(SEQ // tq):
        w0 = i * tq
        outs = []
        for hh in range(2):
            hs = slice(hh * LANES, (hh + 1) * LANES)
            q = q_ref[w0:w0 + tq, hs]
            s_d = lax.dot_general(q, k_ref[w0:w0 + tq, hs], NT, preferred_element_type=F32)
            s_d = jnp.where(causal, s_d, NEG_INF)
            m = jnp.max(s_d, axis=-1, keepdims=True)
            if i > 0:
                s_o = lax.dot_general(q, k_ref[0:w0, hs], NT, preferred_element_type=F32)
                m = jnp.maximum(m, jnp.max(s_o, axis=-1, keepdims=True))
            p_d = jnp.exp2(s_d - m)
            l = jnp.sum(p_d, axis=-1, keepdims=True)
            acc = jnp.dot(p_d.astype(BF16), v_ref[w0:w0 + tq, :], preferred_element_type=F32)
            if i > 0:
                p_o = jnp.exp2(s_o - m)
                l = l + jnp.sum(p_o, axis=-1, keepdims=True)
                acc = acc + jnp.dot(p_o.astype(BF16), v_ref[0:w0, :], preferred_element_type=F32)
            outs.append(acc / l)
        o_ref[w0:w0 + tq, :] = jnp.where(lane < MLA_VDIM, outs[0], outs[1]).astype(o_ref.dtype)


def _mla_attn(q, k, v, *, tq):
    pairs = MLA_HEADS // 2
    return pl.pallas_call(
        functools.partial(_mla_attn_kernel, tq=tq),
        grid=(BATCH, pairs),
        in_specs=[pl.BlockSpec((SEQ, 2 * LANES), lambda b, j: (b, j)),
                  pl.BlockSpec((SEQ, 2 * LANES), lambda b, j: (b, j)),
                  pl.BlockSpec((SEQ, LANES), lambda b, j: (b, j))],
        out_specs=pl.BlockSpec((SEQ, LANES), lambda b, j: (b, j)),
        out_shape=jax.ShapeDtypeStruct((N_PROMPT, MLA_HEADS * MLA_VDIM), BF16),
        compiler_params=_cparams(("parallel", "parallel")), name="mla_attn_prompt",
    )(q, k, v)


def _mla_qlat_kernel(q_ref, w_ref, o_ref):
    for h in range(MLA_HEADS):
        o_ref[h] = jnp.dot(q_ref[:, h * LANES:(h + 1) * LANES], w_ref[h], preferred_element_type=F32)


def _mla_qlat(q_s, w_qlat):
    return pl.pallas_call(
        _mla_qlat_kernel,
        out_shape=jax.ShapeDtypeStruct((MLA_HEADS, DEC_BATCH, MLA_KV_LORA), F32),
        compiler_params=pltpu.CompilerParams(vmem_limit_bytes=VMEM_LIMIT), name="mla_qlat",
    )(q_s, w_qlat)


def _mla_sample_attn_kernel(pt_ref, qlat_ref, qpe_ref, cnew_ref, knew_ref, cnew_row_ref, *rest):
    del pt_ref
    npg = PAGES_PER_STEP
    ckv_refs = rest[:npg]
    kpe_refs = rest[npg:2 * npg]
    o_ref, m_s, l_s, acc_s = rest[2 * npg:]
    b = pl.program_id(0)
    c = pl.program_id(1)
    qlat_t = qlat_ref[...]
    qpe_t = qpe_ref[...]

    @pl.when(c == 0)
    def _():
        pick = lambda ref: jnp.sum(
            jnp.where(lax.broadcasted_iota(jnp.int32, ref.shape, 1) == b, ref[...], 0.0), axis=1, keepdims=True)
        cnew = pick(cnew_ref)
        knew = pick(knew_ref)
        s_new = (jnp.sum(qlat_t * cnew, axis=0, keepdims=True)
                 + jnp.sum(qpe_t * knew, axis=0, keepdims=True))
        m_s[...] = s_new
        l_s[...] = jnp.ones_like(s_new)
        acc_s[...] = jnp.broadcast_to(cnew_row_ref[...], acc_s.shape)

    eye = (lax.broadcasted_iota(jnp.int32, (MLA_HEADS, MLA_HEADS), 0)
           == lax.broadcasted_iota(jnp.int32, (MLA_HEADS, MLA_HEADS), 1))
    to_col = lambda r: jnp.sum(jnp.where(eye, r, 0.0), axis=1, keepdims=True)
    qlat_b = qlat_t.astype(BF16)
    qpe_b = qpe_t.astype(BF16)
    per = npg // PAGE_SPLIT
    m_old = m_s[...]
    m_new = m_old
    parts = []
    for j in range(PAGE_SPLIT):
        ckv = jnp.concatenate([r[...] for r in ckv_refs[j * per:(j + 1) * per]], axis=0).astype(BF16)
        kpe_t = jnp.concatenate([r[...] for r in kpe_refs[j * per:(j + 1) * per]], axis=1).astype(BF16)
        s = (jnp.dot(ckv, qlat_b, preferred_element_type=F32)
             + lax.dot_general(kpe_t, qpe_b, TN, preferred_element_type=F32))
        m_new = jnp.maximum(m_new, jnp.max(s, axis=0, keepdims=True))
        parts.append((s, ckv))
    alpha = jnp.exp2(m_old - m_new)
    l = alpha * l_s[...]
    acc = to_col(alpha) * acc_s[...]
    for s, ckv in parts:
        p = jnp.exp2(s - m_new)
        l = l + jnp.sum(p, axis=0, keepdims=True)
        acc = acc + lax.dot_general(p.astype(BF16), ckv, TN, preferred_element_type=F32)
    m_s[...] = m_new
    l_s[...] = l
    acc_s[...] = acc

    @pl.when(c == pl.num_programs(1) - 1)
    def _():
        o_ref[...] = acc / to_col(l)


def _mla_sample_attn(page_table, qlat_t, qpe_t, cnew_t, knew_t, ckv_pool, kpe_pool_t, layer):
    npg = PAGES_PER_STEP
    nchunk = N_PAGES // npg
    per_b = lambda r: pl.BlockSpec((None, r, MLA_HEADS), lambda b, c, pt: (b, 0, 0))
    whole = lambda r: pl.BlockSpec((r, DEC_BATCH), lambda b, c, pt: (0, 0))
    page = lambda r, w, k: pl.BlockSpec((None, None, r, w), lambda b, c, pt, k=k: (layer, pt[b, c * npg + k], 0, 0))
    in_specs = ([per_b(MLA_KV_LORA), per_b(MLA_ROPE), whole(MLA_KV_LORA), whole(MLA_ROPE),
                 pl.BlockSpec((None, 1, MLA_KV_LORA), lambda b, c, pt: (b, 0, 0))]
                + [page(PAGE_SIZE, MLA_KV_LORA, k) for k in range(npg)]
                + [page(MLA_ROPE, PAGE_SIZE, k) for k in range(npg)])
    grid_spec = pltpu.PrefetchScalarGridSpec(
        num_scalar_prefetch=1, grid=(DEC_BATCH, nchunk), in_specs=in_specs,
        out_specs=pl.BlockSpec((None, MLA_HEADS, MLA_KV_LORA), lambda b, c, pt: (b, 0, 0)),
        scratch_shapes=[pltpu.VMEM((1, MLA_HEADS), F32), pltpu.VMEM((1, MLA_HEADS), F32),
                        pltpu.VMEM((MLA_HEADS, MLA_KV_LORA), F32)])
    return pl.pallas_call(
        _mla_sample_attn_kernel, grid_spec=grid_spec,
        out_shape=jax.ShapeDtypeStruct((DEC_BATCH, MLA_HEADS, MLA_KV_LORA), F32),
        compiler_params=_cparams(("parallel", "arbitrary")), name="mla_attn_sample",
    )(page_table, qlat_t, qpe_t, cnew_t, knew_t, cnew_t.T.reshape(DEC_BATCH, 1, MLA_KV_LORA),
      *([ckv_pool] * npg), *([kpe_pool_t] * npg))


def _mla_sample_out_kernel(olat_ref, w_ref, o_ref):
    for j in range(MLA_HEADS // 2):
        acc = None
        for h in (2 * j, 2 * j + 1):
            t = jnp.dot(olat_ref[h].astype(BF16), w_ref[h], preferred_element_type=F32)
            acc = t if acc is None else acc + t
        o_ref[:, j * LANES:(j + 1) * LANES] = acc.astype(o_ref.dtype)


def _mla_sample_out(olat_t, w_vpair):
    return pl.pallas_call(
        _mla_sample_out_kernel,
        out_shape=jax.ShapeDtypeStruct((DEC_BATCH, MLA_HEADS * MLA_VDIM), BF16),
        compiler_params=pltpu.CompilerParams(vmem_limit_bytes=VMEM_LIMIT), name="mla_sample_out",
    )(olat_t, w_vpair)


def _post_ffn_kernel(*refs, final, ff_chunk):
    if final:
        x_ref, o_ref, wo_ref, g_ref, wup_ref, wdn_ref, gfin_ref, out_ref = refs
    else:
        x_ref, o_ref, wo_ref, g_ref, wup_ref, wdn_ref, out_ref = refs
    x1 = x_ref[...] + jnp.dot(o_ref[...], wo_ref[...], preferred_element_type=F32)
    hn = _rms(x1, g_ref[...]).astype(BF16)
    acc = x1
    for c in range(D_FF // ff_chunk):
        u = jnp.dot(hn, wup_ref[:, c * ff_chunk:(c + 1) * ff_chunk], preferred_element_type=F32)
        u = jnp.square(jnp.maximum(u, 0.0)).astype(BF16)
        acc = acc + jnp.dot(u, wdn_ref[c * ff_chunk:(c + 1) * ff_chunk, :], preferred_element_type=F32)
    if final:
        acc = _rms(acc, gfin_ref[...])
    out_ref[...] = acc


def _post_ffn(x, o, wo, g, wup, wdn, gfin, layer, *, tm, final):
    n = x.shape[0]
    ko = o.shape[1]
    row = lambda w: pl.BlockSpec((tm, w), lambda i: (i, 0))
    per_layer = lambda a: pl.BlockSpec((None,) + a.shape[1:], lambda i: (layer, 0, 0), pipeline_mode=pl.Buffered(1))
    in_specs = [row(D_MODEL), row(ko), _const_spec(wo.shape), _const_spec(g.shape),
                per_layer(wup), per_layer(wdn)]
    args = [x, o, wo, g, wup, wdn]
    if final:
        in_specs.append(_const_spec(gfin.shape))
        args.append(gfin)
    return pl.pallas_call(
        functools.partial(_post_ffn_kernel, final=final, ff_chunk=1024),
        grid=(n // tm,), in_specs=in_specs, out_specs=row(D_MODEL),
        out_shape=jax.ShapeDtypeStruct((n, D_MODEL), F32),
        compiler_params=_cparams(("parallel",)), name="post_ffn_final" if final else "post_ffn",
    )(*args)


def _dil_proj_prompt_kernel(*refs, aliased):
    x_ref, g_ref, w_ref = refs[:3]
    q_out, kv_out, b128_out, b512_out, b2048_out = refs[-5:]
    del aliased
    h = _rms(x_ref[...], g_ref[...]).astype(BF16)
    gw = 3 * DIL_WIDTH
    tm = x_ref.shape[0]
    for g, buf_out in enumerate((b128_out, b512_out, b2048_out)):
        r = jnp.dot(h, w_ref[:, g * gw:(g + 1) * gw], preferred_element_type=F32)
        q_out[g] = r[:, :DIL_WIDTH] * (DIL_HEAD_DIM ** -0.5)
        kv = r[:, DIL_WIDTH:]
        kv_out[g] = kv
        kv_t = kv.T
        wb = buf_out.shape[-1]
        buf_out[...] = kv_t[:, tm - wb:] if wb < tm else kv_t


def _dil_proj_prompt(x, g, w, layer, prev, *, tm):
    n = x.shape[0]
    ng = len(DIL_GROUPS)
    per_b = SEQ // tm
    kvw = 2 * DIL_WIDTH
    buf_shapes, buf_specs = [], []
    for window, _ in DIL_GROUPS:
        wb = min(window, tm)
        nblk = window // wb
        first = per_b - nblk
        buf_shapes.append(jax.ShapeDtypeStruct((DEPTH // 2, BATCH, kvw, window), F32))
        buf_specs.append(pl.BlockSpec(
            (None, None, kvw, wb),
            lambda i, first=first: (layer, i // per_b, 0, jnp.maximum(i % per_b - first, 0))))
    in_specs = [pl.BlockSpec((tm, D_MODEL), lambda i: (i, 0)), _const_spec(g.shape), _const_spec(w.shape)]
    args = [x, g, w]
    aliases = {}
    if prev is not None:
        in_specs += [_any_spec()] * ng
        args += list(prev)
        aliases = {3 + k: 2 + k for k in range(ng)}
    return pl.pallas_call(
        functools.partial(_dil_proj_prompt_kernel, aliased=prev is not None), grid=(n // tm,),
        in_specs=in_specs,
        out_specs=[pl.BlockSpec((ng, tm, DIL_WIDTH), lambda i: (0, i, 0)),
                   pl.BlockSpec((ng, tm, kvw), lambda i: (0, i, 0))] + buf_specs,
        out_shape=[jax.ShapeDtypeStruct((ng, n, DIL_WIDTH), F32), jax.ShapeDtypeStruct((ng, n, kvw), F32)] + buf_shapes,
        input_output_aliases=aliases,
        compiler_params=_cparams(("arbitrary",)), name="dil_proj_prompt",
    )(*args)


def _dil_proj_sample_kernel(x_ref, g_ref, w_ref, q_out, kv_out, kvt_out):
    h = _rms(x_ref[...], g_ref[...]).astype(BF16)
    gw = 3 * DIL_WIDTH
    pad = jnp.zeros((LANES - DEC_BATCH, 2 * DIL_WIDTH), F32)
    for g in range(len(DIL_GROUPS)):
        r = jnp.dot(h, w_ref[:, g * gw:(g + 1) * gw], preferred_element_type=F32)
        q_out[g] = r[:, :DIL_WIDTH] * (DIL_HEAD_DIM ** -0.5)
        kv = r[:, DIL_WIDTH:]
        kv_out[g] = kv
        kvt_out[g] = jnp.concatenate([kv, pad], axis=0).T


def _dil_proj_sample(x, g, w):
    ng = len(DIL_GROUPS)
    return pl.pallas_call(
        _dil_proj_sample_kernel,
        out_shape=[jax.ShapeDtypeStruct((ng, DEC_BATCH, DIL_WIDTH), F32),
                   jax.ShapeDtypeStruct((ng, DEC_BATCH, 2 * DIL_WIDTH), F32),
                   jax.ShapeDtypeStruct((ng, 2 * DIL_WIDTH, LANES), F32)],
        compiler_params=pltpu.CompilerParams(vmem_limit_bytes=VMEM_LIMIT), name="dil_proj_sample",
    )(x, g, w)


def _strided(start, size, stride):
    return pl.ds(start, size) if stride == 1 else pl.ds(start, size, stride=stride)


def _dil_attn_kernel(q_ref, k_ref, v_ref, o_ref, os_ref, ms_ref, ds_ref):
    j = pl.program_id(1)
    sp = DIL_SPAN
    lane = lax.broadcasted_iota(jnp.int32, (sp, LANES), 1)
    first_head = lane < DIL_HEAD_DIM
    jq = lax.broadcasted_iota(jnp.int32, (sp, 2 * sp), 0)
    jk = lax.broadcasted_iota(jnp.int32, (sp, 2 * sp), 1)
    dist = jq + sp - jk
    valid = (dist >= 0) & (dist <= sp)
    slopes = [jnp.exp2(-(8.0 / DIL_HEADS) * (2 * j + hh + 1).astype(F32)) for hh in range(2)]
    sel = lambda a: jnp.where(first_head, a[:sp], a[sp:])
    wide = lambda a: jnp.broadcast_to(a, (2 * sp, LANES))
    own_lanes = ((lax.broadcasted_iota(jnp.int32, (2 * sp, LANES), 1) < DIL_HEAD_DIM)
                 == (lax.broadcasted_iota(jnp.int32, (2 * sp, LANES), 0) < sp))
    for g, (_, d) in enumerate(DIL_GROUPS):
        dist_f = (dist * d).astype(F32)
        bias_full = jnp.concatenate([jnp.where(valid, -slopes[hh] * dist_f, NEG_INF) for hh in range(2)], axis=0)
        bias_cur = bias_full[:, sp:]
        for n in range(SEQ // (sp * d)):
            for r in range(d):
                qs = n * sp * d + r
                rows = _strided(qs, sp, d)
                q = q_ref[g, rows, :]
                k = k_ref[g, rows, :]
                v = v_ref[g, rows, :]
                if n > 0:
                    prev = _strided(qs - sp * d, sp, d)
                    k = jnp.concatenate([k_ref[g, prev, :], k], axis=0)
                    v = jnp.concatenate([v_ref[g, prev, :], v], axis=0)
                q2 = jnp.where(own_lanes, jnp.concatenate([q, q], axis=0), 0.0).astype(BF16)
                s = lax.dot_general(q2, k.astype(BF16), NT, preferred_element_type=F32)
                s = s + (bias_full if n > 0 else bias_cur)
                m = jnp.max(s, axis=-1, keepdims=True)
                e = jnp.exp(s - m)
                den = jnp.sum(e, axis=-1, keepdims=True)
                o = jnp.dot(e.astype(BF16), v.astype(BF16), preferred_element_type=F32)
                os_ref[g, rows, :] = sel(o)
                ms_ref[g, rows, :] = sel(wide(m))
                ds_ref[g, rows, :] = sel(wide(den))
    m_all = jnp.maximum(jnp.maximum(ms_ref[0], ms_ref[1]), ms_ref[2])
    num = jnp.zeros((SEQ, LANES), F32)
    den = jnp.zeros((SEQ, LANES), F32)
    for g in range(len(DIL_GROUPS)):
        w = jnp.exp(ms_ref[g] - m_all)
        num = num + os_ref[g] * w
        den = den + ds_ref[g] * w
    o_ref[...] = (num / den).astype(o_ref.dtype)


def _dil_attn(q, kv):
    ng = len(DIL_GROUPS)
    pairs = DIL_HEADS // 2
    blk = lambda off: pl.BlockSpec((ng, SEQ, LANES), lambda b, j, off=off: (0, b, off + j))
    return pl.pallas_call(
        _dil_attn_kernel, grid=(BATCH, pairs),
        in_specs=[blk(0), blk(0), blk(pairs)],
        out_specs=pl.BlockSpec((SEQ, LANES), lambda b, j: (b, j)),
        out_shape=jax.ShapeDtypeStruct((N_PROMPT, DIL_WIDTH), BF16),
        scratch_shapes=[pltpu.VMEM((ng, SEQ, LANES), F32)] * 3,
        compiler_params=_cparams(("parallel", "parallel")), name="dil_attn_prompt",
    )(q, kv, kv)


def _dil_sample_kernel(*refs, aliased):
    q_ref, kr_ref, vr_ref, kc_ref, vc_ref, s0_ref, s1_ref, s2_ref = refs[:8]
    o_ref, n0_ref, n1_ref, n2_ref = refs[-4:]
    del aliased
    b = pl.program_id(0)
    hb = pl.program_id(1)
    rw = DIL_HEAD_BLOCK * DIL_HEAD_DIM
    rowi = lax.broadcasted_iota(jnp.int32, (8, rw), 0)
    coli = lax.broadcasted_iota(jnp.int32, (8, rw), 1)
    bd = (coli // DIL_HEAD_DIM) == rowi
    head = hb * DIL_HEAD_BLOCK + lax.broadcasted_iota(jnp.int32, (8, 1), 0)
    slope = jnp.exp2(-(8.0 / DIL_HEADS) * (head + 1).astype(F32))
    is_b = lax.broadcasted_iota(jnp.int32, (rw, LANES), 1) == b
    parts = []
    for g, (s_ref, n_ref, (window, d)) in enumerate(zip((s0_ref, s1_ref, s2_ref), (n0_ref, n1_ref, n2_ref), DIL_GROUPS)):
        q_bd = jnp.where(bd, q_ref[g], 0.0)
        k_old = s_ref[0].reshape(rw, window)
        v_old = s_ref[1].reshape(rw, window)
        s = jnp.dot(q_bd.astype(BF16), k_old.astype(BF16), preferred_element_type=F32)
        pos = lax.broadcasted_iota(jnp.int32, (8, window), 1)
        back = (window - pos).astype(F32)
        s = jnp.where((pos & (d - 1)) == 0, s - slope * back, NEG_INF)
        s_new = jnp.sum(q_bd * kr_ref[g], axis=1, keepdims=True)
        m = jnp.maximum(jnp.max(s, axis=1, keepdims=True), s_new)
        e = jnp.exp(s - m)
        e_new = jnp.exp(s_new - m)
        den = jnp.sum(e, axis=1, keepdims=True) + e_new
        o = (lax.dot_general(e.astype(BF16), v_old.astype(BF16), NT, preferred_element_type=F32)
             + e_new * jnp.where(bd, vr_ref[g], 0.0))
        parts.append((o, m, den))
        last = lax.broadcasted_iota(jnp.int32, (rw, window), 1) == window - 1
        for t, (old, c_ref) in enumerate(((k_old, kc_ref), (v_old, vc_ref))):
            col = jnp.sum(jnp.where(is_b, c_ref[g], 0.0), axis=1, keepdims=True)
            new = jnp.where(last, col, pltpu.roll(old, window - 1, 1))
            n_ref[t] = new.reshape(DIL_HEAD_BLOCK, DIL_HEAD_DIM, window)
    m_all = jnp.maximum(jnp.maximum(parts[0][1], parts[1][1]), parts[2][1])
    num = jnp.zeros((8, rw), F32)
    den = jnp.zeros((8, 1), F32)
    for o, m, dn in parts:
        wgt = jnp.exp(m - m_all)
        num = num + o * wgt
        den = den + dn * wgt
    o_ref[...] = jnp.sum(jnp.where(bd, num / den, 0.0), axis=0, keepdims=True).astype(o_ref.dtype)


def _dil_sample(q_s, kv_s, kvt_s, states_t, layer, prev):
    ng = len(DIL_GROUPS)
    w = DIL_WIDTH
    hbk = DIL_HEAD_BLOCK
    rw = hbk * DIL_HEAD_DIM
    nhb = DIL_HEADS // hbk
    row = lambda off: pl.BlockSpec((ng, None, 1, rw), lambda b, hb, off=off: (0, b, 0, off + hb))
    colspec = lambda off: pl.BlockSpec((ng, rw, LANES), lambda b, hb, off=off: (0, off + hb, 0))
    st_spec = lambda window: pl.BlockSpec((None, None, 2, hbk, DIL_HEAD_DIM, window),
                                          lambda b, hb: (layer, b, 0, hb, 0, 0))
    st_specs = [st_spec(window) for window, _ in DIL_GROUPS]
    in_specs = [row(0), row(0), row(nhb), colspec(0), colspec(nhb)] + st_specs
    args = [q_s.reshape(ng, DEC_BATCH, 1, w), kv_s.reshape(ng, DEC_BATCH, 1, 2 * w),
            kv_s.reshape(ng, DEC_BATCH, 1, 2 * w), kvt_s, kvt_s, *states_t]
    aliases = {}
    if prev is not None:
        in_specs += [_any_spec()] * ng
        args += list(prev)
        aliases = {len(args) - ng + k: 1 + k for k in range(ng)}
    return pl.pallas_call(
        functools.partial(_dil_sample_kernel, aliased=prev is not None), grid=(DEC_BATCH, nhb),
        in_specs=in_specs,
        out_specs=[pl.BlockSpec((None, 1, rw), lambda b, hb: (b, 0, hb))] + st_specs,
        out_shape=[jax.ShapeDtypeStruct((DEC_BATCH, 1, w), BF16)]
                  + [jax.ShapeDtypeStruct(st.shape, F32) for st in states_t],
        input_output_aliases=aliases,
        compiler_params=_cparams(("parallel", "parallel")), name="dil_attn_sample",
    )(*args)


def _rope_tables(pos):
    half = MLA_ROPE // 2
    inv = ROPE_THETA ** (-jnp.arange(half, dtype=F32) / half)
    ang = pos.astype(F32)[:, None] * inv[None, :]
    cos, sin = jnp.cos(ang), jnp.sin(ang)
    n = pos.shape[0]
    scale = (MLA_NOPE + MLA_ROPE) ** -0.5 * LOG2E
    zq = jnp.zeros((n, LANES - MLA_NOPE - MLA_ROPE), F32)
    cq = jnp.concatenate([jnp.ones((n, MLA_NOPE), F32), cos, cos, zq], axis=1) * scale
    sq = jnp.concatenate([jnp.zeros((n, MLA_NOPE), F32), -sin, sin, zq], axis=1) * scale
    zk = jnp.zeros((n, LANES - MLA_ROPE), F32)
    ck = jnp.concatenate([cos, cos, zk], axis=1)
    sk = jnp.concatenate([-sin, sin, zk], axis=1)
    return cq, sq, ck, sk


def _mla_weights(w_dq, w_uq, w_dkv, w_uk, w_uv):
    half = MLA_ROPE // 2
    padl = lambda a, wdt: jnp.pad(a, [(0, 0)] * (a.ndim - 1) + [(0, wdt - a.shape[-1])])
    c, r = w_dkv[:, :MLA_KV_LORA], w_dkv[:, MLA_KV_LORA:]
    r_sw = jnp.concatenate([r[:, half:], r[:, :half]], axis=1)
    wa = jnp.concatenate([w_dq, c, padl(r, LANES), padl(r_sw, LANES)], axis=1).astype(BF16)
    uq = w_uq.reshape(MLA_Q_LORA, MLA_HEADS, MLA_NOPE + MLA_ROPE)
    wq1 = padl(uq, LANES).reshape(MLA_Q_LORA, HEAD_PAD).astype(BF16)
    uq_sw = jnp.concatenate([jnp.zeros_like(uq[..., :MLA_NOPE]), uq[..., MLA_NOPE + half:],
                             uq[..., MLA_NOPE:MLA_NOPE + half]], axis=-1)
    wq2 = padl(uq_sw, LANES).reshape(MLA_Q_LORA, HEAD_PAD).astype(BF16)
    wk_top = padl(w_uk, LANES).reshape(MLA_KV_LORA, HEAD_PAD)
    place = jnp.pad(jnp.eye(MLA_ROPE, dtype=F32), ((0, LANES - MLA_ROPE), (MLA_NOPE, LANES - MLA_NOPE - MLA_ROPE)))
    wk = jnp.concatenate([wk_top, jnp.tile(place, (1, MLA_HEADS))], axis=0).astype(BF16)
    wv = w_uv.reshape(MLA_KV_LORA, MLA_HEADS * MLA_VDIM).astype(BF16)
    w_qlat = padl(jnp.transpose(w_uk, (1, 0, 2)), LANES)
    w_qlat = jnp.transpose(w_qlat, (0, 2, 1)).astype(BF16)
    uv = jnp.transpose(w_uv, (1, 0, 2))
    z = jnp.zeros_like(uv)
    even = jnp.concatenate([uv, z], axis=-1)
    odd = jnp.concatenate([z, uv], axis=-1)
    is_even = (jnp.arange(MLA_HEADS) % 2 == 0)[:, None, None]
    w_vpair = jnp.where(is_even, even, odd).astype(BF16)
    return wa, wq1, wq2, wk, wv, w_qlat, w_vpair


def kernel(x_prompt, x_sample, cache_mla_ckv, cache_mla_kpe, page_table, state_dil_kv_w128, state_dil_kv_w512, state_dil_kv_w2048, mla_w_dq, mla_g_q, mla_w_uq, mla_w_dkv, mla_g_kv, mla_w_uk, mla_w_uv, mla_w_o, dil_w_qkv, dil_w_o, norm_mix, norm_ffn, ffn_w_up, ffn_w_down, norm_final):
    tm = 512
    xp = x_prompt.reshape(N_PROMPT, D_MODEL)
    xs = x_sample.reshape(DEC_BATCH, D_MODEL)
    tabs_p = _rope_tables(jnp.arange(SEQ))
    tabs_s = _rope_tables(jnp.full((DEC_BATCH,), PAST_LEN))
    gfin = norm_final.reshape(1, D_MODEL)
    wup = ffn_w_up.astype(BF16)
    wdn = ffn_w_down.astype(BF16)
    kpe_pool_t = jnp.transpose(cache_mla_kpe, (0, 1, 3, 2))
    states_t = [jnp.transpose(st, (0, 1, 3, 4, 5, 2))
                for st in (state_dil_kv_w128, state_dil_kv_w512, state_dil_kv_w2048)]
    ckv_p, kpe_p, ckv_s, kpe_s = [], [], [], []
    dil_p_bufs = None
    dil_s_bufs = None
    for layer in range(DEPTH):
        i = layer // 2
        gmix = norm_mix[layer].reshape(1, D_MODEL)
        if layer % 2 == 0:
            wa, wq1, wq2, wk, wv, w_qlat, w_vpair = _mla_weights(mla_w_dq[i], mla_w_uq[i], mla_w_dkv[i], mla_w_uk[i], mla_w_uv[i])
            gq = mla_g_q[i].reshape(1, MLA_Q_LORA)
            gkv = mla_g_kv[i].reshape(1, MLA_KV_LORA)
            q_p, c_p, k_p, kf_p, v_p = _mla_proj(xp, gmix, wa, gq, wq1, wq2, gkv, tabs_p, wk, wv, tm=tm, with_kv=True)
            q_s, c_s, k_s = _mla_proj(xs, gmix, wa, gq, wq1, wq2, gkv, tabs_s, None, None, tm=DEC_BATCH, with_kv=False)
            o_p = _mla_attn(q_p, kf_p, v_p, tq=512)
            qlat_t = jnp.transpose(_mla_qlat(q_s, w_qlat), (1, 2, 0))
            qpe_t = jnp.transpose(q_s.reshape(DEC_BATCH, MLA_HEADS, LANES)[:, :, MLA_NOPE:MLA_NOPE + MLA_ROPE],
                                  (0, 2, 1)).astype(F32)
            olat = _mla_sample_attn(page_table, qlat_t, qpe_t, c_s.T, k_s.T, cache_mla_ckv, kpe_pool_t, i)
            o_s = _mla_sample_out(jnp.transpose(olat, (1, 0, 2)), w_vpair)
            wo = mla_w_o[i].astype(BF16)
            ckv_p.append(c_p.reshape(BATCH, SEQ, MLA_KV_LORA))
            kpe_p.append(k_p)
            ckv_s.append(c_s.reshape(DEC_BATCH, 1, MLA_KV_LORA))
            kpe_s.append(k_s.reshape(DEC_BATCH, 1, MLA_ROPE))
        else:
            wqkv = dil_w_qkv[i].astype(BF16)
            q_p, kv_p, *dil_p_bufs = _dil_proj_prompt(xp, gmix, wqkv, i, dil_p_bufs, tm=256)
            q_s, kv_s, kvt_s = _dil_proj_sample(xs, gmix, wqkv)
            o_p = _dil_attn(q_p, kv_p)
            o_s, *dil_s_bufs = _dil_sample(q_s, kv_s, kvt_s, states_t, i, dil_s_bufs)
            o_s = o_s.reshape(DEC_BATCH, DIL_WIDTH)
            wo = dil_w_o[i].astype(BF16)
        final = layer == DEPTH - 1
        g_ffn = norm_ffn[layer].reshape(1, D_MODEL)
        xp = _post_ffn(xp, o_p, wo, g_ffn, wup, wdn, gfin, layer, tm=tm, final=final)
        xs = _post_ffn(xs, o_s, wo, g_ffn, wup, wdn, gfin, layer, tm=DEC_BATCH, final=final)
    nl = DEPTH // 2
    to_window = lambda a, nb: jnp.transpose(
        a.reshape(nl, nb, 2, DIL_HEADS, DIL_HEAD_DIM, a.shape[-1]), (0, 1, 5, 2, 3, 4))
    return (xp.reshape(BATCH, SEQ, D_MODEL), xs.reshape(DEC_BATCH, 1, D_MODEL),
            jnp.stack(ckv_p), jnp.transpose(jnp.stack(kpe_p), (0, 1, 3, 2)), jnp.stack(ckv_s), jnp.stack(kpe_s),
            *[to_window(a, BATCH) for a in dil_p_bufs],
            *[to_window(a, DEC_BATCH) for a in dil_s_bufs])
```

```python
import functools
import math

import jax
import jax.numpy as jnp
from jax import lax
from jax.experimental import pallas as pl
from jax.experimental.pallas import tpu as pltpu

F32 = jnp.float32
BF16 = jnp.bfloat16

D_MODEL = 1024
BATCH = 8
SEQ = 2048
DEPTH = 4
DEC_BATCH = 32
PAST_LEN = 16384
PAGE_SIZE = 128
N_PAGES = PAST_LEN // PAGE_SIZE
MLA_HEADS = 16
MLA_NOPE = 64
MLA_ROPE = 32
MLA_VDIM = 64
MLA_Q_LORA = 384
MLA_KV_LORA = 256
ROPE_THETA = 10000.0
DIL_GROUPS = ((128, 1), (512, 4), (2048, 16))
DIL_SPAN = 128
DIL_HEADS = 8
DIL_HEAD_DIM = 64
DIL_WIDTH = DIL_HEADS * DIL_HEAD_DIM
D_FF = 4 * D_MODEL
EPS = 1e-6
NEG_INF = -1e30
LOG2E = math.log2(math.e)

LANES = 128
N_PROMPT = BATCH * SEQ
HEAD_PAD = MLA_HEADS * LANES
VMEM_LIMIT = 56 * 1024 * 1024
PAGES_PER_STEP = 32
PAGE_SPLIT = 16
DIL_HEAD_BLOCK = 4
NT = (((1,), (1,)), ((), ()))
TN = (((0,), (0,)), ((), ()))


def _cparams(sem):
    return pltpu.CompilerParams(dimension_semantics=sem, vmem_limit_bytes=VMEM_LIMIT)


def _rms(x, g):
    return x * lax.rsqrt(jnp.mean(x * x, axis=-1, keepdims=True) + EPS) * g


def _const_spec(shape):
    nd = len(shape)
    return pl.BlockSpec(shape, lambda *_: (0,) * nd, pipeline_mode=pl.Buffered(1))


def _any_spec():
    return pl.BlockSpec(memory_space=pl.ANY)


def _mla_proj_kernel(*refs, with_kv):
    if with_kv:
        (x_ref, gmix_ref, wa_ref, gq_ref, wq1_ref, wq2_ref, gkv_ref, cq_ref, sq_ref, ck_ref, sk_ref,
         wk_ref, wv_ref, q_out, ckv_out, kpe_out, k_out, v_out) = refs
    else:
        (x_ref, gmix_ref, wa_ref, gq_ref, wq1_ref, wq2_ref, gkv_ref, cq_ref, sq_ref, ck_ref, sk_ref,
         q_out, ckv_out, kpe_out) = refs
    h = _rms(x_ref[...], gmix_ref[...]).astype(BF16)
    a = jnp.dot(h, wa_ref[...], preferred_element_type=F32)
    c_q = _rms(a[:, :MLA_Q_LORA], gq_ref[...]).astype(BF16)
    o1 = MLA_Q_LORA + MLA_KV_LORA
    c_kv = _rms(a[:, MLA_Q_LORA:o1], gkv_ref[...])
    k_pe = a[:, o1:o1 + LANES] * ck_ref[...] + a[:, o1 + LANES:o1 + 2 * LANES] * sk_ref[...]
    ckv_out[...] = c_kv
    if with_kv:
        kpe_out[...] = k_pe.T[:MLA_ROPE, :]
    else:
        kpe_out[...] = k_pe[:, :MLA_ROPE]
    heads_per_chunk = 4
    cw = heads_per_chunk * LANES
    cq_c = jnp.concatenate([cq_ref[...]] * heads_per_chunk, axis=1)
    sq_c = jnp.concatenate([sq_ref[...]] * heads_per_chunk, axis=1)
    for c in range(MLA_HEADS // heads_per_chunk):
        q1 = jnp.dot(c_q, wq1_ref[:, c * cw:(c + 1) * cw], preferred_element_type=F32)
        q2 = jnp.dot(c_q, wq2_ref[:, c * cw:(c + 1) * cw], preferred_element_type=F32)
        q_out[:, c * cw:(c + 1) * cw] = (q1 * cq_c + q2 * sq_c).astype(q_out.dtype)
    if with_kv:
        c_kv_b = c_kv.astype(BF16)
        lhs = jnp.concatenate([c_kv_b, k_pe.astype(BF16)], axis=1)
        k_out[...] = jnp.dot(lhs, wk_ref[...], preferred_element_type=F32).astype(k_out.dtype)
        v_out[...] = jnp.dot(c_kv_b, wv_ref[...], preferred_element_type=F32).astype(v_out.dtype)


def _mla_proj(x, gmix, wa, gq, wq1, wq2, gkv, tabs, wk, wv, *, tm, with_kv):
    n = x.shape[0]
    nt = n // tm
    tab_blocks = tabs[0].shape[0] // tm
    row = lambda w: pl.BlockSpec((tm, w), lambda i: (i, 0))
    tab = pl.BlockSpec((tm, LANES), lambda i: (i % tab_blocks, 0))
    in_specs = [row(D_MODEL), _const_spec(gmix.shape), _const_spec(wa.shape), _const_spec(gq.shape),
                _const_spec(wq1.shape), _const_spec(wq2.shape), _const_spec(gkv.shape), tab, tab, tab, tab]
    args = [x, gmix, wa, gq, wq1, wq2, gkv, *tabs]
    out_shape = [jax.ShapeDtypeStruct((n, HEAD_PAD), BF16), jax.ShapeDtypeStruct((n, MLA_KV_LORA), F32)]
    out_specs = [row(HEAD_PAD), row(MLA_KV_LORA)]
    if with_kv:
        per_b = SEQ // tm
        in_specs += [_const_spec(wk.shape), _const_spec(wv.shape)]
        args += [wk, wv]
        out_shape += [jax.ShapeDtypeStruct((BATCH, MLA_ROPE, SEQ), F32),
                      jax.ShapeDtypeStruct((n, HEAD_PAD), BF16),
                      jax.ShapeDtypeStruct((n, MLA_HEADS * MLA_VDIM), BF16)]
        out_specs += [pl.BlockSpec((None, MLA_ROPE, tm), lambda i: (i // per_b, 0, i % per_b)),
                      row(HEAD_PAD), row(MLA_HEADS * MLA_VDIM)]
    else:
        out_shape.append(jax.ShapeDtypeStruct((n, MLA_ROPE), F32))
        out_specs.append(row(MLA_ROPE))
    return pl.pallas_call(
        functools.partial(_mla_proj_kernel, with_kv=with_kv),
        grid=(nt,), in_specs=in_specs, out_specs=out_specs, out_shape=out_shape,
        compiler_params=_cparams(("parallel",)), name="mla_proj_kv" if with_kv else "mla_proj_q",
    )(*args)


def _mla_attn_kernel(q_ref, k_ref, v_ref, o_ref, *fill_refs, tq):
    for r in fill_refs:
        r[...] = jnp.zeros(r.shape, r.dtype)
    row = lax.broadcasted_iota(jnp.int32, (tq, tq), 0)
    col = lax.broadcasted_iota(jnp.int32, (tq, tq), 1)
    causal = col <= row
    lane = lax.broadcasted_iota(jnp.int32, (tq, LANES), 1)
    for i in reversed(range(SEQ // tq)):
        w0 = i * tq
        outs = []
        for hh in range(2):
            hs = slice(hh * LANES, (hh + 1) * LANES)
            q = q_ref[w0:w0 + tq, hs]
            s_d = lax.dot_general(q, k_ref[w0:w0 + tq, hs], NT, preferred_element_type=F32)
            s_d = jnp.where(causal, s_d, NEG_INF)
            m = jnp.max(s_d, axis=-1, keepdims=True)
            if i > 0:
                s_o = lax.dot_general(q, k_ref[0:w0, hs], NT, preferred_element_type=F32)
                m = jnp.maximum(m, jnp.max(s_o, axis=-1, keepdims=True))
            p_d = jnp.exp2(s_d - m)
            l = jnp.sum(p_d, axis=-1, keepdims=True)
            acc = jnp.dot(p_d.astype(BF16), v_ref[w0:w0 + tq, :], preferred_element_type=F32)
            if i > 0:
                p_o = jnp.exp2(s_o - m)
                l = l + jnp.sum(p_o, axis=-1, keepdims=True)
                acc = acc + jnp.dot(p_o.astype(BF16), v_ref[0:w0, :], preferred_element_type=F32)
            outs.append(acc / l)
        o_ref[w0:w0 + tq, :] = jnp.where(lane < MLA_VDIM, outs[0], outs[1]).astype(o_ref.dtype)


def _window_fill_specs():
    nl = DEPTH // 2
    pairs = MLA_HEADS // 2
    kvw = 2 * DIL_WIDTH
    shapes, specs = [], []
    for window, _ in DIL_GROUPS:
        shapes.append(jax.ShapeDtypeStruct((nl, BATCH, kvw, window), F32))
        specs.append(pl.BlockSpec((nl, None, kvw // pairs, window), lambda b, j: (0, b, j, 0)))
    per_b = DEC_BATCH // BATCH
    for window, _ in DIL_GROUPS:
        shapes.append(jax.ShapeDtypeStruct((nl, DEC_BATCH, 2, DIL_HEADS, DIL_HEAD_DIM, window), F32))
        specs.append(pl.BlockSpec((nl, per_b, 2, None, DIL_HEAD_DIM, window), lambda b, j: (0, b, 0, j, 0, 0)))
    return shapes, specs


def _mla_attn(q, k, v, *, tq, fill):
    pairs = MLA_HEADS // 2
    assert pairs == DIL_HEADS
    fill_shapes, fill_specs = _window_fill_specs() if fill else ([], [])
    return pl.pallas_call(
        functools.partial(_mla_attn_kernel, tq=tq),
        grid=(BATCH, pairs),
        in_specs=[pl.BlockSpec((SEQ, 2 * LANES), lambda b, j: (b, j)),
                  pl.BlockSpec((SEQ, 2 * LANES), lambda b, j: (b, j)),
                  pl.BlockSpec((SEQ, LANES), lambda b, j: (b, j))],
        out_specs=[pl.BlockSpec((SEQ, LANES), lambda b, j: (b, j))] + fill_specs,
        out_shape=[jax.ShapeDtypeStruct((N_PROMPT, MLA_HEADS * MLA_VDIM), BF16)] + fill_shapes,
        compiler_params=_cparams(("parallel", "parallel")), name="mla_attn_prompt",
    )(q, k, v)


def _mla_qlat_kernel(q_ref, w_ref, o_ref):
    for h in range(MLA_HEADS):
        o_ref[h] = jnp.dot(q_ref[:, h * LANES:(h + 1) * LANES], w_ref[h], preferred_element_type=F32)


def _mla_qlat(q_s, w_qlat):
    return pl.pallas_call(
        _mla_qlat_kernel,
        out_shape=jax.ShapeDtypeStruct((MLA_HEADS, DEC_BATCH, MLA_KV_LORA), F32),
        compiler_params=pltpu.CompilerParams(vmem_limit_bytes=VMEM_LIMIT), name="mla_qlat",
    )(q_s, w_qlat)


def _mla_sample_attn_kernel(pt_ref, qlat_ref, qpe_ref, cnew_ref, knew_ref, cnew_row_ref, *rest):
    del pt_ref
    npg = PAGES_PER_STEP
    ckv_refs = rest[:npg]
    kpe_refs = rest[npg:2 * npg]
    o_ref, m_s, l_s, acc_s = rest[2 * npg:]
    b = pl.program_id(0)
    c = pl.program_id(1)
    qlat_t = qlat_ref[...]
    qpe_t = qpe_ref[...]

    @pl.when(c == 0)
    def _():
        pick = lambda ref: jnp.sum(
            jnp.where(lax.broadcasted_iota(jnp.int32, ref.shape, 1) == b, ref[...], 0.0), axis=1, keepdims=True)
        cnew = pick(cnew_ref)
        knew = pick(knew_ref)
        s_new = (jnp.sum(qlat_t * cnew, axis=0, keepdims=True)
                 + jnp.sum(qpe_t * knew, axis=0, keepdims=True))
        m_s[...] = s_new
        l_s[...] = jnp.ones_like(s_new)
        acc_s[...] = jnp.broadcast_to(cnew_row_ref[...], acc_s.shape)

    eye = (lax.broadcasted_iota(jnp.int32, (MLA_HEADS, MLA_HEADS), 0)
           == lax.broadcasted_iota(jnp.int32, (MLA_HEADS, MLA_HEADS), 1))
    to_col = lambda r: jnp.sum(jnp.where(eye, r, 0.0), axis=1, keepdims=True)
    qlat_b = qlat_t.astype(BF16)
    qpe_b = qpe_t.astype(BF16)
    per = npg // PAGE_SPLIT
    m_old = m_s[...]
    m_new = m_old
    parts = []
    for j in range(PAGE_SPLIT):
        ckv = jnp.concatenate([r[...] for r in ckv_refs[j * per:(j + 1) * per]], axis=0).astype(BF16)
        kpe_t = jnp.concatenate([r[...] for r in kpe_refs[j * per:(j + 1) * per]], axis=1).astype(BF16)
        s = (jnp.dot(ckv, qlat_b, preferred_element_type=F32)
             + lax.dot_general(kpe_t, qpe_b, TN, preferred_element_type=F32))
        m_new = jnp.maximum(m_new, jnp.max(s, axis=0, keepdims=True))
        parts.append((s, ckv))
    alpha = jnp.exp2(m_old - m_new)
    l = alpha * l_s[...]
    acc = to_col(alpha) * acc_s[...]
    for s, ckv in parts:
        p = jnp.exp2(s - m_new)
        l = l + jnp.sum(p, axis=0, keepdims=True)
        acc = acc + lax.dot_general(p.astype(BF16), ckv, TN, preferred_element_type=F32)
    m_s[...] = m_new
    l_s[...] = l
    acc_s[...] = acc

    @pl.when(c == pl.num_programs(1) - 1)
    def _():
        o_ref[...] = acc / to_col(l)


def _mla_sample_attn(page_table, qlat_t, qpe_t, cnew_t, knew_t, ckv_pool, kpe_pool_t, layer):
    npg = PAGES_PER_STEP
    nchunk = N_PAGES // npg
    per_b = lambda r: pl.BlockSpec((None, r, MLA_HEADS), lambda b, c, pt: (b, 0, 0))
    whole = lambda r: pl.BlockSpec((r, DEC_BATCH), lambda b, c, pt: (0, 0))
    page = lambda r, w, k: pl.BlockSpec((None, None, r, w), lambda b, c, pt, k=k: (layer, pt[b, c * npg + k], 0, 0))
    in_specs = ([per_b(MLA_KV_LORA), per_b(MLA_ROPE), whole(MLA_KV_LORA), whole(MLA_ROPE),
                 pl.BlockSpec((None, 1, MLA_KV_LORA), lambda b, c, pt: (b, 0, 0))]
                + [page(PAGE_SIZE, MLA_KV_LORA, k) for k in range(npg)]
                + [page(MLA_ROPE, PAGE_SIZE, k) for k in range(npg)])
    grid_spec = pltpu.PrefetchScalarGridSpec(
        num_scalar_prefetch=1, grid=(DEC_BATCH, nchunk), in_specs=in_specs,
        out_specs=pl.BlockSpec((None, MLA_HEADS, MLA_KV_LORA), lambda b, c, pt: (b, 0, 0)),
        scratch_shapes=[pltpu.VMEM((1, MLA_HEADS), F32), pltpu.VMEM((1, MLA_HEADS), F32),
                        pltpu.VMEM((MLA_HEADS, MLA_KV_LORA), F32)])
    return pl.pallas_call(
        _mla_sample_attn_kernel, grid_spec=grid_spec,
        out_shape=jax.ShapeDtypeStruct((DEC_BATCH, MLA_HEADS, MLA_KV_LORA), F32),
        compiler_params=_cparams(("parallel", "arbitrary")), name="mla_attn_sample",
    )(page_table, qlat_t, qpe_t, cnew_t, knew_t, cnew_t.T.reshape(DEC_BATCH, 1, MLA_KV_LORA),
      *([ckv_pool] * npg), *([kpe_pool_t] * npg))


def _mla_sample_out_kernel(olat_ref, w_ref, o_ref):
    for j in range(MLA_HEADS // 2):
        acc = None
        for h in (2 * j, 2 * j + 1):
            t = jnp.dot(olat_ref[h].astype(BF16), w_ref[h], preferred_element_type=F32)
            acc = t if acc is None else acc + t
        o_ref[:, j * LANES:(j + 1) * LANES] = acc.astype(o_ref.dtype)


def _mla_sample_out(olat_t, w_vpair):
    return pl.pallas_call(
        _mla_sample_out_kernel,
        out_shape=jax.ShapeDtypeStruct((DEC_BATCH, MLA_HEADS * MLA_VDIM), BF16),
        compiler_params=pltpu.CompilerParams(vmem_limit_bytes=VMEM_LIMIT), name="mla_sample_out",
    )(olat_t, w_vpair)


def _post_ffn_kernel(*refs, final, ff_chunk):
    if final:
        x_ref, o_ref, wo_ref, g_ref, wup_ref, wdn_ref, gfin_ref, out_ref = refs
    else:
        x_ref, o_ref, wo_ref, g_ref, wup_ref, wdn_ref, out_ref = refs
    x1 = x_ref[...] + jnp.dot(o_ref[...], wo_ref[...], preferred_element_type=F32)
    hn = _rms(x1, g_ref[...]).astype(BF16)
    acc = x1
    for c in range(D_FF // ff_chunk):
        u = jnp.dot(hn, wup_ref[:, c * ff_chunk:(c + 1) * ff_chunk], preferred_element_type=F32)
        u = jnp.square(jnp.maximum(u, 0.0)).astype(BF16)
        acc = acc + jnp.dot(u, wdn_ref[c * ff_chunk:(c + 1) * ff_chunk, :], preferred_element_type=F32)
    if final:
        acc = _rms(acc, gfin_ref[...])
    out_ref[...] = acc


def _post_ffn(x, o, wo, g, wup, wdn, gfin, layer, *, tm, final):
    n = x.shape[0]
    ko = o.shape[1]
    row = lambda w: pl.BlockSpec((tm, w), lambda i: (i, 0))
    per_layer = lambda a: pl.BlockSpec((None,) + a.shape[1:], lambda i: (layer, 0, 0), pipeline_mode=pl.Buffered(1))
    in_specs = [row(D_MODEL), row(ko), _const_spec(wo.shape), _const_spec(g.shape),
                per_layer(wup), per_layer(wdn)]
    args = [x, o, wo, g, wup, wdn]
    if final:
        in_specs.append(_const_spec(gfin.shape))
        args.append(gfin)
    return pl.pallas_call(
        functools.partial(_post_ffn_kernel, final=final, ff_chunk=1024),
        grid=(n // tm,), in_specs=in_specs, out_specs=row(D_MODEL),
        out_shape=jax.ShapeDtypeStruct((n, D_MODEL), F32),
        compiler_params=_cparams(("parallel",)), name="post_ffn_final" if final else "post_ffn",
    )(*args)


def _dil_proj_prompt_kernel(*refs):
    x_ref, g_ref, w_ref = refs[:3]
    q_out, kv_out, b128_out, b512_out, b2048_out = refs[-5:]
    h = _rms(x_ref[...], g_ref[...]).astype(BF16)
    gw = 3 * DIL_WIDTH
    tm = x_ref.shape[0]
    for g, buf_out in enumerate((b128_out, b512_out, b2048_out)):
        r = jnp.dot(h, w_ref[:, g * gw:(g + 1) * gw], preferred_element_type=F32)
        q_out[g] = r[:, :DIL_WIDTH] * (DIL_HEAD_DIM ** -0.5)
        kv = r[:, DIL_WIDTH:]
        kv_out[g] = kv
        kv_t = kv.T
        wb = buf_out.shape[-1]
        buf_out[...] = kv_t[:, tm - wb:] if wb < tm else kv_t


def _dil_proj_prompt(x, g, w, layer, prev, *, tm):
    n = x.shape[0]
    ng = len(DIL_GROUPS)
    per_b = SEQ // tm
    kvw = 2 * DIL_WIDTH
    buf_shapes, buf_specs = [], []
    for window, _ in DIL_GROUPS:
        wb = min(window, tm)
        nblk = window // wb
        first = per_b - nblk
        buf_shapes.append(jax.ShapeDtypeStruct((DEPTH // 2, BATCH, kvw, window), F32))
        buf_specs.append(pl.BlockSpec(
            (None, None, kvw, wb),
            lambda i, first=first: (layer, i // per_b, 0, jnp.maximum(i % per_b - first, 0))))
    in_specs = [pl.BlockSpec((tm, D_MODEL), lambda i: (i, 0)), _const_spec(g.shape), _const_spec(w.shape)]
    in_specs += [_any_spec()] * ng
    args = [x, g, w, *prev]
    aliases = {3 + k: 2 + k for k in range(ng)}
    return pl.pallas_call(
        _dil_proj_prompt_kernel, grid=(n // tm,),
        in_specs=in_specs,
        out_specs=[pl.BlockSpec((ng, tm, DIL_WIDTH), lambda i: (0, i, 0)),
                   pl.BlockSpec((ng, tm, kvw), lambda i: (0, i, 0))] + buf_specs,
        out_shape=[jax.ShapeDtypeStruct((ng, n, DIL_WIDTH), F32), jax.ShapeDtypeStruct((ng, n, kvw), F32)] + buf_shapes,
        input_output_aliases=aliases,
        compiler_params=_cparams(("arbitrary",)), name="dil_proj_prompt",
    )(*args)


def _dil_proj_sample_kernel(x_ref, g_ref, w_ref, q_out, kv_out, kvt_out):
    h = _rms(x_ref[...], g_ref[...]).astype(BF16)
    gw = 3 * DIL_WIDTH
    pad = jnp.zeros((LANES - DEC_BATCH, 2 * DIL_WIDTH), F32)
    for g in range(len(DIL_GROUPS)):
        r = jnp.dot(h, w_ref[:, g * gw:(g + 1) * gw], preferred_element_type=F32)
        q_out[g] = r[:, :DIL_WIDTH] * (DIL_HEAD_DIM ** -0.5)
        kv = r[:, DIL_WIDTH:]
        kv_out[g] = kv
        kvt_out[g] = jnp.concatenate([kv, pad], axis=0).T


def _dil_proj_sample(x, g, w):
    ng = len(DIL_GROUPS)
    return pl.pallas_call(
        _dil_proj_sample_kernel,
        out_shape=[jax.ShapeDtypeStruct((ng, DEC_BATCH, DIL_WIDTH), F32),
                   jax.ShapeDtypeStruct((ng, DEC_BATCH, 2 * DIL_WIDTH), F32),
                   jax.ShapeDtypeStruct((ng, 2 * DIL_WIDTH, LANES), F32)],
        compiler_params=pltpu.CompilerParams(vmem_limit_bytes=VMEM_LIMIT), name="dil_proj_sample",
    )(x, g, w)


def _strided(start, size, stride):
    return pl.ds(start, size) if stride == 1 else pl.ds(start, size, stride=stride)


def _dil_attn_kernel(q_ref, k_ref, v_ref, o_ref, os_ref, ms_ref, ds_ref):
    j = pl.program_id(1)
    sp = DIL_SPAN
    lane = lax.broadcasted_iota(jnp.int32, (sp, LANES), 1)
    first_head = lane < DIL_HEAD_DIM
    jq = lax.broadcasted_iota(jnp.int32, (sp, 2 * sp), 0)
    jk = lax.broadcasted_iota(jnp.int32, (sp, 2 * sp), 1)
    dist = jq + sp - jk
    valid = (dist >= 0) & (dist <= sp)
    slopes = [jnp.exp2(-(8.0 / DIL_HEADS) * (2 * j + hh + 1).astype(F32)) for hh in range(2)]
    sel = lambda a: jnp.where(first_head, a[:sp], a[sp:])
    wide = lambda a: jnp.broadcast_to(a, (2 * sp, LANES))
    own_lanes = ((lax.broadcasted_iota(jnp.int32, (2 * sp, LANES), 1) < DIL_HEAD_DIM)
                 == (lax.broadcasted_iota(jnp.int32, (2 * sp, LANES), 0) < sp))
    for g, (_, d) in enumerate(DIL_GROUPS):
        dist_f = (dist * d).astype(F32)
        bias_full = jnp.concatenate([jnp.where(valid, -slopes[hh] * dist_f, NEG_INF) for hh in range(2)], axis=0)
        bias_cur = bias_full[:, sp:]
        for n in range(SEQ // (sp * d)):
            for r in range(d):
                qs = n * sp * d + r
                rows = _strided(qs, sp, d)
                q = q_ref[g, rows, :]
                k = k_ref[g, rows, :]
                v = v_ref[g, rows, :]
                if n > 0:
                    prev = _strided(qs - sp * d, sp, d)
                    k = jnp.concatenate([k_ref[g, prev, :], k], axis=0)
                    v = jnp.concatenate([v_ref[g, prev, :], v], axis=0)
                q2 = jnp.where(own_lanes, jnp.concatenate([q, q], axis=0), 0.0).astype(BF16)
                s = lax.dot_general(q2, k.astype(BF16), NT, preferred_element_type=F32)
                s = s + (bias_full if n > 0 else bias_cur)
                m = jnp.max(s, axis=-1, keepdims=True)
                e = jnp.exp(s - m)
                den = jnp.sum(e, axis=-1, keepdims=True)
                o = jnp.dot(e.astype(BF16), v.astype(BF16), preferred_element_type=F32)
                os_ref[g, rows, :] = sel(o)
                ms_ref[g, rows, :] = sel(wide(m))
                ds_ref[g, rows, :] = sel(wide(den))
    m_all = jnp.maximum(jnp.maximum(ms_ref[0], ms_ref[1]), ms_ref[2])
    num = jnp.zeros((SEQ, LANES), F32)
    den = jnp.zeros((SEQ, LANES), F32)
    for g in range(len(DIL_GROUPS)):
        w = jnp.exp(ms_ref[g] - m_all)
        num = num + os_ref[g] * w
        den = den + ds_ref[g] * w
    o_ref[...] = (num / den).astype(o_ref.dtype)


def _dil_attn(q, kv):
    ng = len(DIL_GROUPS)
    pairs = DIL_HEADS // 2
    blk = lambda off: pl.BlockSpec((ng, SEQ, LANES), lambda b, j, off=off: (0, b, off + j))
    return pl.pallas_call(
        _dil_attn_kernel, grid=(BATCH, pairs),
        in_specs=[blk(0), blk(0), blk(pairs)],
        out_specs=pl.BlockSpec((SEQ, LANES), lambda b, j: (b, j)),
        out_shape=jax.ShapeDtypeStruct((N_PROMPT, DIL_WIDTH), BF16),
        scratch_shapes=[pltpu.VMEM((ng, SEQ, LANES), F32)] * 3,
        compiler_params=_cparams(("parallel", "parallel")), name="dil_attn_prompt",
    )(q, kv, kv)


def _dil_sample_kernel(*refs):
    q_ref, kr_ref, vr_ref, kc_ref, vc_ref, s0_ref, s1_ref, s2_ref = refs[:8]
    o_ref, n0_ref, n1_ref, n2_ref = refs[-4:]
    b = pl.program_id(0)
    hb = pl.program_id(1)
    rw = DIL_HEAD_BLOCK * DIL_HEAD_DIM
    rowi = lax.broadcasted_iota(jnp.int32, (8, rw), 0)
    coli = lax.broadcasted_iota(jnp.int32, (8, rw), 1)
    bd = (coli // DIL_HEAD_DIM) == rowi
    head = hb * DIL_HEAD_BLOCK + lax.broadcasted_iota(jnp.int32, (8, 1), 0)
    slope = jnp.exp2(-(8.0 / DIL_HEADS) * (head + 1).astype(F32))
    is_b = lax.broadcasted_iota(jnp.int32, (rw, LANES), 1) == b
    parts = []
    for g, (s_ref, n_ref, (window, d)) in enumerate(zip((s0_ref, s1_ref, s2_ref), (n0_ref, n1_ref, n2_ref), DIL_GROUPS)):
        q_bd = jnp.where(bd, q_ref[g], 0.0)
        k_old = s_ref[0].reshape(rw, window)
        v_old = s_ref[1].reshape(rw, window)
        s = jnp.dot(q_bd.astype(BF16), k_old.astype(BF16), preferred_element_type=F32)
        pos = lax.broadcasted_iota(jnp.int32, (8, window), 1)
        back = (window - pos).astype(F32)
        s = jnp.where((pos & (d - 1)) == 0, s - slope * back, NEG_INF)
        s_new = jnp.sum(q_bd * kr_ref[g], axis=1, keepdims=True)
        m = jnp.maximum(jnp.max(s, axis=1, keepdims=True), s_new)
        e = jnp.exp(s - m)
        e_new = jnp.exp(s_new - m)
        den = jnp.sum(e, axis=1, keepdims=True) + e_new
        o = (lax.dot_general(e.astype(BF16), v_old.astype(BF16), NT, preferred_element_type=F32)
             + e_new * jnp.where(bd, vr_ref[g], 0.0))
        parts.append((o, m, den))
        last = lax.broadcasted_iota(jnp.int32, (rw, window), 1) == window - 1
        for t, (old, c_ref) in enumerate(((k_old, kc_ref), (v_old, vc_ref))):
            col = jnp.sum(jnp.where(is_b, c_ref[g], 0.0), axis=1, keepdims=True)
            new = jnp.where(last, col, pltpu.roll(old, window - 1, 1))
            n_ref[t] = new.reshape(DIL_HEAD_BLOCK, DIL_HEAD_DIM, window)
    m_all = jnp.maximum(jnp.maximum(parts[0][1], parts[1][1]), parts[2][1])
    num = jnp.zeros((8, rw), F32)
    den = jnp.zeros((8, 1), F32)
    for o, m, dn in parts:
        wgt = jnp.exp(m - m_all)
        num = num + o * wgt
        den = den + dn * wgt
    o_ref[...] = jnp.sum(jnp.where(bd, num / den, 0.0), axis=0, keepdims=True).astype(o_ref.dtype)


def _dil_sample(q_s, kv_s, kvt_s, states_t, layer, prev):
    ng = len(DIL_GROUPS)
    w = DIL_WIDTH
    hbk = DIL_HEAD_BLOCK
    rw = hbk * DIL_HEAD_DIM
    nhb = DIL_HEADS // hbk
    row = lambda off: pl.BlockSpec((ng, None, 1, rw), lambda b, hb, off=off: (0, b, 0, off + hb))
    colspec = lambda off: pl.BlockSpec((ng, rw, LANES), lambda b, hb, off=off: (0, off + hb, 0))
    st_spec = lambda window: pl.BlockSpec((None, None, 2, hbk, DIL_HEAD_DIM, window),
                                          lambda b, hb: (layer, b, 0, hb, 0, 0))
    st_specs = [st_spec(window) for window, _ in DIL_GROUPS]
    in_specs = [row(0), row(0), row(nhb), colspec(0), colspec(nhb)] + st_specs
    args = [q_s.reshape(ng, DEC_BATCH, 1, w), kv_s.reshape(ng, DEC_BATCH, 1, 2 * w),
            kv_s.reshape(ng, DEC_BATCH, 1, 2 * w), kvt_s, kvt_s, *states_t]
    aliases = {len(args) + k: 1 + k for k in range(ng)}
    in_specs += [_any_spec()] * ng
    args += list(prev)
    return pl.pallas_call(
        _dil_sample_kernel, grid=(DEC_BATCH, nhb),
        in_specs=in_specs,
        out_specs=[pl.BlockSpec((None, 1, rw), lambda b, hb: (b, 0, hb))] + st_specs,
        out_shape=[jax.ShapeDtypeStruct((DEC_BATCH, 1, w), BF16)]
                  + [jax.ShapeDtypeStruct(st.shape, F32) for st in states_t],
        input_output_aliases=aliases,
        compiler_params=_cparams(("parallel", "parallel")), name="dil_attn_sample",
    )(*args)


def _rope_tables(pos):
    half = MLA_ROPE // 2
    inv = ROPE_THETA ** (-jnp.arange(half, dtype=F32) / half)
    ang = pos.astype(F32)[:, None] * inv[None, :]
    cos, sin = jnp.cos(ang), jnp.sin(ang)
    n = pos.shape[0]
    scale = (MLA_NOPE + MLA_ROPE) ** -0.5 * LOG2E
    zq = jnp.zeros((n, LANES - MLA_NOPE - MLA_ROPE), F32)
    cq = jnp.concatenate([jnp.ones((n, MLA_NOPE), F32), cos, cos, zq], axis=1) * scale
    sq = jnp.concatenate([jnp.zeros((n, MLA_NOPE), F32), -sin, sin, zq], axis=1) * scale
    zk = jnp.zeros((n, LANES - MLA_ROPE), F32)
    ck = jnp.concatenate([cos, cos, zk], axis=1)
    sk = jnp.concatenate([-sin, sin, zk], axis=1)
    return cq, sq, ck, sk


def _mla_weights(w_dq, w_uq, w_dkv, w_uk, w_uv):
    half = MLA_ROPE // 2
    padl = lambda a, wdt: jnp.pad(a, [(0, 0)] * (a.ndim - 1) + [(0, wdt - a.shape[-1])])
    c, r = w_dkv[:, :MLA_KV_LORA], w_dkv[:, MLA_KV_LORA:]
    r_sw = jnp.concatenate([r[:, half:], r[:, :half]], axis=1)
    wa = jnp.concatenate([w_dq, c, padl(r, LANES), padl(r_sw, LANES)], axis=1).astype(BF16)
    uq = w_uq.reshape(MLA_Q_LORA, MLA_HEADS, MLA_NOPE + MLA_ROPE)
    wq1 = padl(uq, LANES).reshape(MLA_Q_LORA, HEAD_PAD).astype(BF16)
    uq_sw = jnp.concatenate([jnp.zeros_like(uq[..., :MLA_NOPE]), uq[..., MLA_NOPE + half:],
                             uq[..., MLA_NOPE:MLA_NOPE + half]], axis=-1)
    wq2 = padl(uq_sw, LANES).reshape(MLA_Q_LORA, HEAD_PAD).astype(BF16)
    wk_top = padl(w_uk, LANES).reshape(MLA_KV_LORA, HEAD_PAD)
    place = jnp.pad(jnp.eye(MLA_ROPE, dtype=F32), ((0, LANES - MLA_ROPE), (MLA_NOPE, LANES - MLA_NOPE - MLA_ROPE)))
    wk = jnp.concatenate([wk_top, jnp.tile(place, (1, MLA_HEADS))], axis=0).astype(BF16)
    wv = w_uv.reshape(MLA_KV_LORA, MLA_HEADS * MLA_VDIM).astype(BF16)
    w_qlat = padl(jnp.transpose(w_uk, (1, 0, 2)), LANES)
    w_qlat = jnp.transpose(w_qlat, (0, 2, 1)).astype(BF16)
    uv = jnp.transpose(w_uv, (1, 0, 2))
    z = jnp.zeros_like(uv)
    even = jnp.concatenate([uv, z], axis=-1)
    odd = jnp.concatenate([z, uv], axis=-1)
    is_even = (jnp.arange(MLA_HEADS) % 2 == 0)[:, None, None]
    w_vpair = jnp.where(is_even, even, odd).astype(BF16)
    return wa, wq1, wq2, wk, wv, w_qlat, w_vpair


def kernel(x_prompt, x_sample, cache_mla_ckv, cache_mla_kpe, page_table, state_dil_kv_w128, state_dil_kv_w512, state_dil_kv_w2048, mla_w_dq, mla_g_q, mla_w_uq, mla_w_dkv, mla_g_kv, mla_w_uk, mla_w_uv, mla_w_o, dil_w_qkv, dil_w_o, norm_mix, norm_ffn, ffn_w_up, ffn_w_down, norm_final):
    tm = 512
    xp = x_prompt.reshape(N_PROMPT, D_MODEL)
    xs = x_sample.reshape(DEC_BATCH, D_MODEL)
    tabs_p = _rope_tables(jnp.arange(SEQ))
    tabs_s = _rope_tables(jnp.full((DEC_BATCH,), PAST_LEN))
    gfin = norm_final.reshape(1, D_MODEL)
    wup = ffn_w_up.astype(BF16)
    wdn = ffn_w_down.astype(BF16)
    kpe_pool_t = jnp.transpose(cache_mla_kpe, (0, 1, 3, 2))
    states_t = [jnp.transpose(st, (0, 1, 3, 4, 5, 2))
                for st in (state_dil_kv_w128, state_dil_kv_w512, state_dil_kv_w2048)]
    ckv_p, kpe_p, ckv_s, kpe_s = [], [], [], []
    for layer in range(DEPTH):
        i = layer // 2
        gmix = norm_mix[layer].reshape(1, D_MODEL)
        if layer % 2 == 0:
            wa, wq1, wq2, wk, wv, w_qlat, w_vpair = _mla_weights(mla_w_dq[i], mla_w_uq[i], mla_w_dkv[i], mla_w_uk[i], mla_w_uv[i])
            gq = mla_g_q[i].reshape(1, MLA_Q_LORA)
            gkv = mla_g_kv[i].reshape(1, MLA_KV_LORA)
            q_p, c_p, k_p, kf_p, v_p = _mla_proj(xp, gmix, wa, gq, wq1, wq2, gkv, tabs_p, wk, wv, tm=tm, with_kv=True)
            q_s, c_s, k_s = _mla_proj(xs, gmix, wa, gq, wq1, wq2, gkv, tabs_s, None, None, tm=DEC_BATCH, with_kv=False)
            if layer == 0:
                o_p, *fills = _mla_attn(q_p, kf_p, v_p, tq=512, fill=True)
                dil_p_bufs, dil_s_bufs = fills[:len(DIL_GROUPS)], fills[len(DIL_GROUPS):]
            else:
                (o_p,) = _mla_attn(q_p, kf_p, v_p, tq=512, fill=False)
            qlat_t = jnp.transpose(_mla_qlat(q_s, w_qlat), (1, 2, 0))
            qpe_t = jnp.transpose(q_s.reshape(DEC_BATCH, MLA_HEADS, LANES)[:, :, MLA_NOPE:MLA_NOPE + MLA_ROPE],
                                  (0, 2, 1)).astype(F32)
            olat = _mla_sample_attn(page_table, qlat_t, qpe_t, c_s.T, k_s.T, cache_mla_ckv, kpe_pool_t, i)
            o_s = _mla_sample_out(jnp.transpose(olat, (1, 0, 2)), w_vpair)
            wo = mla_w_o[i].astype(BF16)
            ckv_p.append(c_p.reshape(BATCH, SEQ, MLA_KV_LORA))
            kpe_p.append(k_p)
            ckv_s.append(c_s.reshape(DEC_BATCH, 1, MLA_KV_LORA))
            kpe_s.append(k_s.reshape(DEC_BATCH, 1, MLA_ROPE))
        else:
            wqkv = dil_w_qkv[i].astype(BF16)
            q_p, kv_p, *dil_p_bufs = _dil_proj_prompt(xp, gmix, wqkv, i, dil_p_bufs, tm=256)
            q_s, kv_s, kvt_s = _dil_proj_sample(xs, gmix, wqkv)
            o_p = _dil_attn(q_p, kv_p)
            o_s, *dil_s_bufs = _dil_sample(q_s, kv_s, kvt_s, states_t, i, dil_s_bufs)
            o_s = o_s.reshape(DEC_BATCH, DIL_WIDTH)
            wo = dil_w_o[i].astype(BF16)
        final = layer == DEPTH - 1
        g_ffn = norm_ffn[layer].reshape(1, D_MODEL)
        xp = _post_ffn(xp, o_p, wo, g_ffn, wup, wdn, gfin, layer, tm=tm, final=final)
        xs = _post_ffn(xs, o_s, wo, g_ffn, wup, wdn, gfin, layer, tm=DEC_BATCH, final=final)
    nl = DEPTH // 2
    to_window = lambda a, nb: jnp.transpose(
        a.reshape(nl, nb, 2, DIL_HEADS, DIL_HEAD_DIM, a.shape[-1]), (0, 1, 5, 2, 3, 4))
    return (xp.reshape(BATCH, SEQ, D_MODEL), xs.reshape(DEC_BATCH, 1, D_MODEL),
            jnp.stack(ckv_p), jnp.transpose(jnp.stack(kpe_p), (0, 1, 3, 2)), jnp.stack(ckv_s), jnp.stack(kpe_s),
            *[to_window(a, BATCH) for a in dil_p_bufs],
            *[to_window(a, DEC_BATCH) for a in dil_s_bufs])
```

```python
import functools
import math

import jax
import jax.numpy as jnp
from jax import lax
from jax.experimental import pallas as pl
from jax.experimental.pallas import tpu as pltpu

F32 = jnp.float32
BF16 = jnp.bfloat16

D_MODEL = 1024
BATCH = 8
SEQ = 2048
DEPTH = 4
DEC_BATCH = 32
PAST_LEN = 16384
PAGE_SIZE = 128
N_PAGES = PAST_LEN // PAGE_SIZE
MLA_HEADS = 16
MLA_NOPE = 64
MLA_ROPE = 32
MLA_VDIM = 64
MLA_Q_LORA = 384
MLA_KV_LORA = 256
ROPE_THETA = 10000.0
DIL_GROUPS = ((128, 1), (512, 4), (2048, 16))
DIL_SPAN = 128
DIL_HEADS = 8
DIL_HEAD_DIM = 64
DIL_WIDTH = DIL_HEADS * DIL_HEAD_DIM
D_FF = 4 * D_MODEL
EPS = 1e-6
NEG_INF = -1e30
LOG2E = math.log2(math.e)

LANES = 128
N_PROMPT = BATCH * SEQ
HEAD_PAD = MLA_HEADS * LANES
VMEM_LIMIT = 56 * 1024 * 1024
PAGES_PER_STEP = 32
FUSED_PAGES_PER_STEP = 64
PAGES_PER_CHUNK = 2
DIL_HEAD_BLOCK = 4
NT = (((1,), (1,)), ((), ()))
TN = (((0,), (0,)), ((), ()))


def _cparams(sem):
    return pltpu.CompilerParams(dimension_semantics=sem, vmem_limit_bytes=VMEM_LIMIT)


def _rms(x, g):
    return x * lax.rsqrt(jnp.mean(x * x, axis=-1, keepdims=True) + EPS) * g


def _const_spec(shape):
    nd = len(shape)
    return pl.BlockSpec(shape, lambda *_: (0,) * nd, pipeline_mode=pl.Buffered(1))


def _any_spec():
    return pl.BlockSpec(memory_space=pl.ANY)


def _mla_proj_kernel(*refs, with_kv):
    if with_kv:
        (x_ref, gmix_ref, wa_ref, gq_ref, wq1_ref, wq2_ref, gkv_ref, cq_ref, sq_ref, ck_ref, sk_ref,
         wk_ref, wv_ref, q_out, ckv_out, kpe_out, k_out, v_out) = refs
    else:
        (x_ref, gmix_ref, wa_ref, gq_ref, wq1_ref, wq2_ref, gkv_ref, cq_ref, sq_ref, ck_ref, sk_ref,
         q_out, ckv_out, kpe_out) = refs
    h = _rms(x_ref[...], gmix_ref[...]).astype(BF16)
    a = jnp.dot(h, wa_ref[...], preferred_element_type=F32)
    c_q = _rms(a[:, :MLA_Q_LORA], gq_ref[...]).astype(BF16)
    o1 = MLA_Q_LORA + MLA_KV_LORA
    c_kv = _rms(a[:, MLA_Q_LORA:o1], gkv_ref[...])
    k_pe = a[:, o1:o1 + LANES] * ck_ref[...] + a[:, o1 + LANES:o1 + 2 * LANES] * sk_ref[...]
    ckv_out[...] = c_kv
    if with_kv:
        kpe_out[...] = k_pe.T[:MLA_ROPE, :]
    else:
        kpe_out[...] = k_pe[:, :MLA_ROPE]
    heads_per_chunk = 4
    cw = heads_per_chunk * LANES
    cq_c = jnp.concatenate([cq_ref[...]] * heads_per_chunk, axis=1)
    sq_c = jnp.concatenate([sq_ref[...]] * heads_per_chunk, axis=1)
    for c in range(MLA_HEADS // heads_per_chunk):
        q1 = jnp.dot(c_q, wq1_ref[:, c * cw:(c + 1) * cw], preferred_element_type=F32)
        q2 = jnp.dot(c_q, wq2_ref[:, c * cw:(c + 1) * cw], preferred_element_type=F32)
        q_out[:, c * cw:(c + 1) * cw] = (q1 * cq_c + q2 * sq_c).astype(q_out.dtype)
    if with_kv:
        c_kv_b = c_kv.astype(BF16)
        lhs = jnp.concatenate([c_kv_b, k_pe.astype(BF16)], axis=1)
        k_out[...] = jnp.dot(lhs, wk_ref[...], preferred_element_type=F32).astype(k_out.dtype)
        v_out[...] = jnp.dot(c_kv_b, wv_ref[...], preferred_element_type=F32).astype(v_out.dtype)


def _mla_proj(x, gmix, wa, gq, wq1, wq2, gkv, tabs, wk, wv, *, tm, with_kv):
    n = x.shape[0]
    nt = n // tm
    tab_blocks = tabs[0].shape[0] // tm
    row = lambda w: pl.BlockSpec((tm, w), lambda i: (i, 0))
    tab = pl.BlockSpec((tm, LANES), lambda i: (i % tab_blocks, 0))
    in_specs = [row(D_MODEL), _const_spec(gmix.shape), _const_spec(wa.shape), _const_spec(gq.shape),
                _const_spec(wq1.shape), _const_spec(wq2.shape), _const_spec(gkv.shape), tab, tab, tab, tab]
    args = [x, gmix, wa, gq, wq1, wq2, gkv, *tabs]
    out_shape = [jax.ShapeDtypeStruct((n, HEAD_PAD), BF16), jax.ShapeDtypeStruct((n, MLA_KV_LORA), F32)]
    out_specs = [row(HEAD_PAD), row(MLA_KV_LORA)]
    if with_kv:
        per_b = SEQ // tm
        in_specs += [_const_spec(wk.shape), _const_spec(wv.shape)]
        args += [wk, wv]
        out_shape += [jax.ShapeDtypeStruct((BATCH, MLA_ROPE, SEQ), F32),
                      jax.ShapeDtypeStruct((n, HEAD_PAD), BF16),
                      jax.ShapeDtypeStruct((n, MLA_HEADS * MLA_VDIM), BF16)]
        out_specs += [pl.BlockSpec((None, MLA_ROPE, tm), lambda i: (i // per_b, 0, i % per_b)),
                      row(HEAD_PAD), row(MLA_HEADS * MLA_VDIM)]
    else:
        out_shape.append(jax.ShapeDtypeStruct((n, MLA_ROPE), F32))
        out_specs.append(row(MLA_ROPE))
    return pl.pallas_call(
        functools.partial(_mla_proj_kernel, with_kv=with_kv),
        grid=(nt,), in_specs=in_specs, out_specs=out_specs, out_shape=out_shape,
        compiler_params=_cparams(("parallel",)), name="mla_proj_kv" if with_kv else "mla_proj_q",
    )(*args)


def _prompt_attn_body(q_ref, k_ref, v_ref, o_ref, tq):
    row = lax.broadcasted_iota(jnp.int32, (tq, tq), 0)
    col = lax.broadcasted_iota(jnp.int32, (tq, tq), 1)
    causal = col <= row
    lane = lax.broadcasted_iota(jnp.int32, (tq, LANES), 1)
    for i in reversed(range(SEQ // tq)):
        w0 = i * tq
        outs = []
        for hh in range(2):
            hs = slice(hh * LANES, (hh + 1) * LANES)
            q = q_ref[w0:w0 + tq, hs]
            s_d = lax.dot_general(q, k_ref[w0:w0 + tq, hs], NT, preferred_element_type=F32)
            s_d = jnp.where(causal, s_d, NEG_INF)
            m = jnp.max(s_d, axis=-1, keepdims=True)
            if i > 0:
                s_o = lax.dot_general(q, k_ref[0:w0, hs], NT, preferred_element_type=F32)
                m = jnp.maximum(m, jnp.max(s_o, axis=-1, keepdims=True))
            p_d = jnp.exp2(s_d - m)
            l = jnp.sum(p_d, axis=-1, keepdims=True)
            acc = jnp.dot(p_d.astype(BF16), v_ref[w0:w0 + tq, :], preferred_element_type=F32)
            if i > 0:
                p_o = jnp.exp2(s_o - m)
                l = l + jnp.sum(p_o, axis=-1, keepdims=True)
                acc = acc + jnp.dot(p_o.astype(BF16), v_ref[0:w0, :], preferred_element_type=F32)
            outs.append(acc / l)
        o_ref[w0:w0 + tq, :] = jnp.where(lane < MLA_VDIM, outs[0], outs[1]).astype(o_ref.dtype)


def _mla_attn_kernel(q_ref, k_ref, v_ref, o_ref, *fill_refs, tq):
    for r in fill_refs:
        r[...] = jnp.zeros(r.shape, r.dtype)
    _prompt_attn_body(q_ref, k_ref, v_ref, o_ref, tq)


def _window_fill_specs():
    nl = DEPTH // 2
    pairs = MLA_HEADS // 2
    kvw = 2 * DIL_WIDTH
    shapes, specs = [], []
    for window, _ in DIL_GROUPS:
        shapes.append(jax.ShapeDtypeStruct((nl, BATCH, kvw, window), F32))
        specs.append(pl.BlockSpec((nl, None, kvw // pairs, window), lambda b, j: (0, b, j, 0)))
    per_b = DEC_BATCH // BATCH
    for window, _ in DIL_GROUPS:
        shapes.append(jax.ShapeDtypeStruct((nl, DEC_BATCH, 2, DIL_HEADS, DIL_HEAD_DIM, window), F32))
        specs.append(pl.BlockSpec((nl, per_b, 2, None, DIL_HEAD_DIM, window), lambda b, j: (0, b, 0, j, 0, 0)))
    return shapes, specs


def _mla_attn(q, k, v, *, tq, fill):
    pairs = MLA_HEADS // 2
    assert pairs == DIL_HEADS
    fill_shapes, fill_specs = _window_fill_specs() if fill else ([], [])
    return pl.pallas_call(
        functools.partial(_mla_attn_kernel, tq=tq),
        grid=(BATCH, pairs),
        in_specs=[pl.BlockSpec((SEQ, 2 * LANES), lambda b, j: (b, j)),
                  pl.BlockSpec((SEQ, 2 * LANES), lambda b, j: (b, j)),
                  pl.BlockSpec((SEQ, LANES), lambda b, j: (b, j))],
        out_specs=[pl.BlockSpec((SEQ, LANES), lambda b, j: (b, j))] + fill_specs,
        out_shape=[jax.ShapeDtypeStruct((N_PROMPT, MLA_HEADS * MLA_VDIM), BF16)] + fill_shapes,
        compiler_params=_cparams(("parallel", "parallel")), name="mla_attn_prompt",
    )(q, k, v)


def _mla_qlat_kernel(q_ref, w_ref, o_ref):
    for h in range(MLA_HEADS):
        o_ref[h] = jnp.dot(q_ref[:, h * LANES:(h + 1) * LANES], w_ref[h], preferred_element_type=F32)


def _mla_qlat(q_s, w_qlat):
    return pl.pallas_call(
        _mla_qlat_kernel,
        out_shape=jax.ShapeDtypeStruct((MLA_HEADS, DEC_BATCH, MLA_KV_LORA), F32),
        compiler_params=pltpu.CompilerParams(vmem_limit_bytes=VMEM_LIMIT), name="mla_qlat",
    )(q_s, w_qlat)


def _sample_attn_init(batch, qlat_ref, qpe_ref, cnew_ref, knew_ref, cnew_row_ref, m_s, l_s, acc_s):
    pick = lambda ref: jnp.sum(
        jnp.where(lax.broadcasted_iota(jnp.int32, ref.shape, 1) == batch, ref[...], 0.0), axis=1, keepdims=True)
    s_new = (jnp.sum(qlat_ref[...] * pick(cnew_ref), axis=0, keepdims=True)
             + jnp.sum(qpe_ref[...] * pick(knew_ref), axis=0, keepdims=True))
    m_s[...] = s_new
    l_s[...] = jnp.ones_like(s_new)
    acc_s[...] = jnp.broadcast_to(cnew_row_ref[...], acc_s.shape)


def _heads_to_col(r):
    eye = (lax.broadcasted_iota(jnp.int32, (MLA_HEADS, MLA_HEADS), 0)
           == lax.broadcasted_iota(jnp.int32, (MLA_HEADS, MLA_HEADS), 1))
    return jnp.sum(jnp.where(eye, r, 0.0), axis=1, keepdims=True)


def _sample_attn_pages(qlat_ref, qpe_ref, ckv_refs, kpe_refs, m_s, l_s, acc_s, per):
    qlat_b = qlat_ref[...].astype(BF16)
    qpe_b = qpe_ref[...].astype(BF16)
    m_old = m_s[...]
    m_new = m_old
    parts = []
    for j in range(len(ckv_refs) // per):
        ckv = jnp.concatenate([r[...] for r in ckv_refs[j * per:(j + 1) * per]], axis=0).astype(BF16)
        kpe_t = jnp.concatenate([r[...] for r in kpe_refs[j * per:(j + 1) * per]], axis=1).astype(BF16)
        s = (jnp.dot(ckv, qlat_b, preferred_element_type=F32)
             + lax.dot_general(kpe_t, qpe_b, TN, preferred_element_type=F32))
        m_new = jnp.maximum(m_new, jnp.max(s, axis=0, keepdims=True))
        parts.append((s, ckv))
    alpha = jnp.exp2(m_old - m_new)
    l = alpha * l_s[...]
    acc = _heads_to_col(alpha) * acc_s[...]
    for s, ckv in parts:
        p = jnp.exp2(s - m_new)
        l = l + jnp.sum(p, axis=0, keepdims=True)
        acc = acc + lax.dot_general(p.astype(BF16), ckv, TN, preferred_element_type=F32)
    m_s[...] = m_new
    l_s[...] = l
    acc_s[...] = acc
    return l, acc


def _mla_sample_attn_kernel(pt_ref, qlat_ref, qpe_ref, cnew_ref, knew_ref, cnew_row_ref, *rest):
    del pt_ref
    npg = PAGES_PER_STEP
    ckv_refs = rest[:npg]
    kpe_refs = rest[npg:2 * npg]
    o_ref, m_s, l_s, acc_s = rest[2 * npg:]
    c = pl.program_id(1)

    @pl.when(c == 0)
    def _():
        _sample_attn_init(pl.program_id(0), qlat_ref, qpe_ref, cnew_ref, knew_ref, cnew_row_ref, m_s, l_s, acc_s)

    l, acc = _sample_attn_pages(qlat_ref, qpe_ref, ckv_refs, kpe_refs, m_s, l_s, acc_s, PAGES_PER_CHUNK)

    @pl.when(c == pl.num_programs(1) - 1)
    def _():
        o_ref[...] = acc / _heads_to_col(l)


def _mla_fused_attn_kernel(pt_ref, q_ref, k_ref, v_ref, qlat_ref, qpe_ref, cnew_ref, knew_ref, cnew_row_ref,
                           *rest, tq, npg):
    del pt_ref
    ckv_refs = rest[:npg]
    kpe_refs = rest[npg:2 * npg]
    o_ref, olat_ref, m_s, l_s, acc_s = rest[2 * npg:]
    t = pl.program_id(0) * pl.num_programs(1) + pl.program_id(1)
    steps_per_row = N_PAGES // npg

    @pl.when(t % steps_per_row == 0)
    def _():
        _sample_attn_init(t // steps_per_row, qlat_ref, qpe_ref, cnew_ref, knew_ref, cnew_row_ref, m_s, l_s, acc_s)

    l, acc = _sample_attn_pages(qlat_ref, qpe_ref, ckv_refs, kpe_refs, m_s, l_s, acc_s, PAGES_PER_CHUNK)
    _prompt_attn_body(q_ref, k_ref, v_ref, o_ref, tq)

    @pl.when(t % steps_per_row == steps_per_row - 1)
    def _():
        olat_ref[...] = acc / _heads_to_col(l)


def _mla_fused_attn(q, k, v, page_table, qlat_t, qpe_t, cnew_t, knew_t, ckv_pool, kpe_pool_t, layer, *, tq):
    pairs = MLA_HEADS // 2
    npg = FUSED_PAGES_PER_STEP
    steps_per_row = N_PAGES // npg
    assert BATCH * pairs * npg == DEC_BATCH * N_PAGES
    srow = lambda b, j: (b * pairs + j) // steps_per_row
    prompt = lambda w: pl.BlockSpec((SEQ, w), lambda b, j, pt: (b, j))
    per_row = lambda r, w: pl.BlockSpec((None, r, w), lambda b, j, pt: (srow(b, j), 0, 0))
    whole = lambda r: pl.BlockSpec((r, DEC_BATCH), lambda b, j, pt: (0, 0))
    page = lambda r, w, kk: pl.BlockSpec(
        (None, None, r, w),
        lambda b, j, pt, kk=kk: (layer, pt[srow(b, j), ((b * pairs + j) % steps_per_row) * npg + kk], 0, 0))
    in_specs = ([prompt(2 * LANES), prompt(2 * LANES), prompt(LANES),
                 per_row(MLA_KV_LORA, MLA_HEADS), per_row(MLA_ROPE, MLA_HEADS), whole(MLA_KV_LORA), whole(MLA_ROPE),
                 per_row(1, MLA_KV_LORA)]
                + [page(PAGE_SIZE, MLA_KV_LORA, kk) for kk in range(npg)]
                + [page(MLA_ROPE, PAGE_SIZE, kk) for kk in range(npg)])
    grid_spec = pltpu.PrefetchScalarGridSpec(
        num_scalar_prefetch=1, grid=(BATCH, pairs), in_specs=in_specs,
        out_specs=[pl.BlockSpec((SEQ, LANES), lambda b, j, pt: (b, j)),
                   pl.BlockSpec((None, MLA_HEADS, MLA_KV_LORA), lambda b, j, pt: (srow(b, j), 0, 0))],
        scratch_shapes=[pltpu.VMEM((1, MLA_HEADS), F32), pltpu.VMEM((1, MLA_HEADS), F32),
                        pltpu.VMEM((MLA_HEADS, MLA_KV_LORA), F32)])
    return pl.pallas_call(
        functools.partial(_mla_fused_attn_kernel, tq=tq, npg=npg), grid_spec=grid_spec,
        out_shape=[jax.ShapeDtypeStruct((N_PROMPT, MLA_HEADS * MLA_VDIM), BF16),
                   jax.ShapeDtypeStruct((DEC_BATCH, MLA_HEADS, MLA_KV_LORA), F32)],
        compiler_params=_cparams(("arbitrary", "arbitrary")), name="mla_attn_fused",
    )(page_table, q, k, v, qlat_t, qpe_t, cnew_t, knew_t, cnew_t.T.reshape(DEC_BATCH, 1, MLA_KV_LORA),
      *([ckv_pool] * npg), *([kpe_pool_t] * npg))


def _mla_sample_attn(page_table, qlat_t, qpe_t, cnew_t, knew_t, ckv_pool, kpe_pool_t, layer):
    npg = PAGES_PER_STEP
    nchunk = N_PAGES // npg
    per_b = lambda r: pl.BlockSpec((None, r, MLA_HEADS), lambda b, c, pt: (b, 0, 0))
    whole = lambda r: pl.BlockSpec((r, DEC_BATCH), lambda b, c, pt: (0, 0))
    page = lambda r, w, k: pl.BlockSpec((None, None, r, w), lambda b, c, pt, k=k: (layer, pt[b, c * npg + k], 0, 0))
    in_specs = ([per_b(MLA_KV_LORA), per_b(MLA_ROPE), whole(MLA_KV_LORA), whole(MLA_ROPE),
                 pl.BlockSpec((None, 1, MLA_KV_LORA), lambda b, c, pt: (b, 0, 0))]
                + [page(PAGE_SIZE, MLA_KV_LORA, k) for k in range(npg)]
                + [page(MLA_ROPE, PAGE_SIZE, k) for k in range(npg)])
    grid_spec = pltpu.PrefetchScalarGridSpec(
        num_scalar_prefetch=1, grid=(DEC_BATCH, nchunk), in_specs=in_specs,
        out_specs=pl.BlockSpec((None, MLA_HEADS, MLA_KV_LORA), lambda b, c, pt: (b, 0, 0)),
        scratch_shapes=[pltpu.VMEM((1, MLA_HEADS), F32), pltpu.VMEM((1, MLA_HEADS), F32),
                        pltpu.VMEM((MLA_HEADS, MLA_KV_LORA), F32)])
    return pl.pallas_call(
        _mla_sample_attn_kernel, grid_spec=grid_spec,
        out_shape=jax.ShapeDtypeStruct((DEC_BATCH, MLA_HEADS, MLA_KV_LORA), F32),
        compiler_params=_cparams(("parallel", "arbitrary")), name="mla_attn_sample",
    )(page_table, qlat_t, qpe_t, cnew_t, knew_t, cnew_t.T.reshape(DEC_BATCH, 1, MLA_KV_LORA),
      *([ckv_pool] * npg), *([kpe_pool_t] * npg))


def _mla_sample_out_kernel(olat_ref, w_ref, o_ref):
    for j in range(MLA_HEADS // 2):
        acc = None
        for h in (2 * j, 2 * j + 1):
            t = jnp.dot(olat_ref[h].astype(BF16), w_ref[h], preferred_element_type=F32)
            acc = t if acc is None else acc + t
        o_ref[:, j * LANES:(j + 1) * LANES] = acc.astype(o_ref.dtype)


def _mla_sample_out(olat_t, w_vpair):
    return pl.pallas_call(
        _mla_sample_out_kernel,
        out_shape=jax.ShapeDtypeStruct((DEC_BATCH, MLA_HEADS * MLA_VDIM), BF16),
        compiler_params=pltpu.CompilerParams(vmem_limit_bytes=VMEM_LIMIT), name="mla_sample_out",
    )(olat_t, w_vpair)


def _post_ffn_kernel(*refs, final, ff_chunk):
    if final:
        x_ref, o_ref, wo_ref, g_ref, wup_ref, wdn_ref, gfin_ref, out_ref = refs
    else:
        x_ref, o_ref, wo_ref, g_ref, wup_ref, wdn_ref, out_ref = refs
    x1 = x_ref[...] + jnp.dot(o_ref[...], wo_ref[...], preferred_element_type=F32)
    hn = _rms(x1, g_ref[...]).astype(BF16)
    acc = x1
    for c in range(D_FF // ff_chunk):
        u = jnp.dot(hn, wup_ref[:, c * ff_chunk:(c + 1) * ff_chunk], preferred_element_type=F32)
        u = jnp.square(jnp.maximum(u, 0.0)).astype(BF16)
        acc = acc + jnp.dot(u, wdn_ref[c * ff_chunk:(c + 1) * ff_chunk, :], preferred_element_type=F32)
    if final:
        acc = _rms(acc, gfin_ref[...])
    out_ref[...] = acc


def _post_ffn(x, o, wo, g, wup, wdn, gfin, layer, *, tm, final):
    n = x.shape[0]
    ko = o.shape[1]
    row = lambda w: pl.BlockSpec((tm, w), lambda i: (i, 0))
    per_layer = lambda a: pl.BlockSpec((None,) + a.shape[1:], lambda i: (layer, 0, 0), pipeline_mode=pl.Buffered(1))
    in_specs = [row(D_MODEL), row(ko), _const_spec(wo.shape), _const_spec(g.shape),
                per_layer(wup), per_layer(wdn)]
    args = [x, o, wo, g, wup, wdn]
    if final:
        in_specs.append(_const_spec(gfin.shape))
        args.append(gfin)
    return pl.pallas_call(
        functools.partial(_post_ffn_kernel, final=final, ff_chunk=1024),
        grid=(n // tm,), in_specs=in_specs, out_specs=row(D_MODEL),
        out_shape=jax.ShapeDtypeStruct((n, D_MODEL), F32),
        compiler_params=_cparams(("parallel",)), name="post_ffn_final" if final else "post_ffn",
    )(*args)


def _dil_proj_prompt_kernel(*refs):
    x_ref, g_ref, w_ref = refs[:3]
    q_out, kv_out, b128_out, b512_out, b2048_out = refs[-5:]
    h = _rms(x_ref[...], g_ref[...]).astype(BF16)
    gw = 3 * DIL_WIDTH
    tm = x_ref.shape[0]
    for g, buf_out in enumerate((b128_out, b512_out, b2048_out)):
        r = jnp.dot(h, w_ref[:, g * gw:(g + 1) * gw], preferred_element_type=F32)
        q_out[g] = r[:, :DIL_WIDTH] * (DIL_HEAD_DIM ** -0.5)
        kv = r[:, DIL_WIDTH:]
        kv_out[g] = kv
        kv_t = kv.T
        wb = buf_out.shape[-1]
        buf_out[...] = kv_t[:, tm - wb:] if wb < tm else kv_t


def _dil_proj_prompt(x, g, w, layer, prev, *, tm):
    n = x.shape[0]
    ng = len(DIL_GROUPS)
    per_b = SEQ // tm
    kvw = 2 * DIL_WIDTH
    buf_shapes, buf_specs = [], []
    for window, _ in DIL_GROUPS:
        wb = min(window, tm)
        nblk = window // wb
        first = per_b - nblk
        buf_shapes.append(jax.ShapeDtypeStruct((DEPTH // 2, BATCH, kvw, window), F32))
        buf_specs.append(pl.BlockSpec(
            (None, None, kvw, wb),
            lambda i, first=first: (layer, i // per_b, 0, jnp.maximum(i % per_b - first, 0))))
    in_specs = [pl.BlockSpec((tm, D_MODEL), lambda i: (i, 0)), _const_spec(g.shape), _const_spec(w.shape)]
    in_specs += [_any_spec()] * ng
    args = [x, g, w, *prev]
    aliases = {3 + k: 2 + k for k in range(ng)}
    return pl.pallas_call(
        _dil_proj_prompt_kernel, grid=(n // tm,),
        in_specs=in_specs,
        out_specs=[pl.BlockSpec((ng, tm, DIL_WIDTH), lambda i: (0, i, 0)),
                   pl.BlockSpec((ng, tm, kvw), lambda i: (0, i, 0))] + buf_specs,
        out_shape=[jax.ShapeDtypeStruct((ng, n, DIL_WIDTH), F32), jax.ShapeDtypeStruct((ng, n, kvw), F32)] + buf_shapes,
        input_output_aliases=aliases,
        compiler_params=_cparams(("arbitrary",)), name="dil_proj_prompt",
    )(*args)


def _dil_proj_sample_kernel(x_ref, g_ref, w_ref, q_out, kv_out, kvt_out):
    h = _rms(x_ref[...], g_ref[...]).astype(BF16)
    gw = 3 * DIL_WIDTH
    pad = jnp.zeros((LANES - DEC_BATCH, 2 * DIL_WIDTH), F32)
    for g in range(len(DIL_GROUPS)):
        r = jnp.dot(h, w_ref[:, g * gw:(g + 1) * gw], preferred_element_type=F32)
        q_out[g] = r[:, :DIL_WIDTH] * (DIL_HEAD_DIM ** -0.5)
        kv = r[:, DIL_WIDTH:]
        kv_out[g] = kv
        kvt_out[g] = jnp.concatenate([kv, pad], axis=0).T


def _dil_proj_sample(x, g, w):
    ng = len(DIL_GROUPS)
    return pl.pallas_call(
        _dil_proj_sample_kernel,
        out_shape=[jax.ShapeDtypeStruct((ng, DEC_BATCH, DIL_WIDTH), F32),
                   jax.ShapeDtypeStruct((ng, DEC_BATCH, 2 * DIL_WIDTH), F32),
                   jax.ShapeDtypeStruct((ng, 2 * DIL_WIDTH, LANES), F32)],
        compiler_params=pltpu.CompilerParams(vmem_limit_bytes=VMEM_LIMIT), name="dil_proj_sample",
    )(x, g, w)


def _strided(start, size, stride):
    return pl.ds(start, size) if stride == 1 else pl.ds(start, size, stride=stride)


def _dil_attn_kernel(q_ref, k_ref, v_ref, o_ref, os_ref, ms_ref, ds_ref):
    j = pl.program_id(1)
    sp = DIL_SPAN
    lane = lax.broadcasted_iota(jnp.int32, (sp, LANES), 1)
    first_head = lane < DIL_HEAD_DIM
    jq = lax.broadcasted_iota(jnp.int32, (sp, 2 * sp), 0)
    jk = lax.broadcasted_iota(jnp.int32, (sp, 2 * sp), 1)
    dist = jq + sp - jk
    valid = (dist >= 0) & (dist <= sp)
    slopes = [jnp.exp2(-(8.0 / DIL_HEADS) * (2 * j + hh + 1).astype(F32)) for hh in range(2)]
    sel = lambda a: jnp.where(first_head, a[:sp], a[sp:])
    wide = lambda a: jnp.broadcast_to(a, (2 * sp, LANES))
    own_lanes = ((lax.broadcasted_iota(jnp.int32, (2 * sp, LANES), 1) < DIL_HEAD_DIM)
                 == (lax.broadcasted_iota(jnp.int32, (2 * sp, LANES), 0) < sp))
    for g, (_, d) in enumerate(DIL_GROUPS):
        dist_f = (dist * d).astype(F32)
        bias_full = jnp.concatenate([jnp.where(valid, -slopes[hh] * dist_f, NEG_INF) for hh in range(2)], axis=0)
        bias_cur = bias_full[:, sp:]
        for n in range(SEQ // (sp * d)):
            for r in range(d):
                qs = n * sp * d + r
                rows = _strided(qs, sp, d)
                q = q_ref[g, rows, :]
                k = k_ref[g, rows, :]
                v = v_ref[g, rows, :]
                if n > 0:
                    prev = _strided(qs - sp * d, sp, d)
                    k = jnp.concatenate([k_ref[g, prev, :], k], axis=0)
                    v = jnp.concatenate([v_ref[g, prev, :], v], axis=0)
                q2 = jnp.where(own_lanes, jnp.concatenate([q, q], axis=0), 0.0).astype(BF16)
                s = lax.dot_general(q2, k.astype(BF16), NT, preferred_element_type=F32)
                s = s + (bias_full if n > 0 else bias_cur)
                m = jnp.max(s, axis=-1, keepdims=True)
                e = jnp.exp(s - m)
                den = jnp.sum(e, axis=-1, keepdims=True)
                o = jnp.dot(e.astype(BF16), v.astype(BF16), preferred_element_type=F32)
                os_ref[g, rows, :] = sel(o)
                ms_ref[g, rows, :] = sel(wide(m))
                ds_ref[g, rows, :] = sel(wide(den))
    m_all = jnp.maximum(jnp.maximum(ms_ref[0], ms_ref[1]), ms_ref[2])
    num = jnp.zeros((SEQ, LANES), F32)
    den = jnp.zeros((SEQ, LANES), F32)
    for g in range(len(DIL_GROUPS)):
        w = jnp.exp(ms_ref[g] - m_all)
        num = num + os_ref[g] * w
        den = den + ds_ref[g] * w
    o_ref[...] = (num / den).astype(o_ref.dtype)


def _dil_attn(q, kv):
    ng = len(DIL_GROUPS)
    pairs = DIL_HEADS // 2
    blk = lambda off: pl.BlockSpec((ng, SEQ, LANES), lambda b, j, off=off: (0, b, off + j))
    return pl.pallas_call(
        _dil_attn_kernel, grid=(BATCH, pairs),
        in_specs=[blk(0), blk(0), blk(pairs)],
        out_specs=pl.BlockSpec((SEQ, LANES), lambda b, j: (b, j)),
        out_shape=jax.ShapeDtypeStruct((N_PROMPT, DIL_WIDTH), BF16),
        scratch_shapes=[pltpu.VMEM((ng, SEQ, LANES), F32)] * 3,
        compiler_params=_cparams(("parallel", "parallel")), name="dil_attn_prompt",
    )(q, kv, kv)


def _dil_sample_kernel(*refs):
    q_ref, kr_ref, vr_ref, kc_ref, vc_ref, s0_ref, s1_ref, s2_ref = refs[:8]
    o_ref, n0_ref, n1_ref, n2_ref = refs[-4:]
    b = pl.program_id(0)
    hb = pl.program_id(1)
    rw = DIL_HEAD_BLOCK * DIL_HEAD_DIM
    rowi = lax.broadcasted_iota(jnp.int32, (8, rw), 0)
    coli = lax.broadcasted_iota(jnp.int32, (8, rw), 1)
    bd = (coli // DIL_HEAD_DIM) == rowi
    head = hb * DIL_HEAD_BLOCK + lax.broadcasted_iota(jnp.int32, (8, 1), 0)
    slope = jnp.exp2(-(8.0 / DIL_HEADS) * (head + 1).astype(F32))
    is_b = lax.broadcasted_iota(jnp.int32, (rw, LANES), 1) == b
    parts = []
    for g, (s_ref, n_ref, (window, d)) in enumerate(zip((s0_ref, s1_ref, s2_ref), (n0_ref, n1_ref, n2_ref), DIL_GROUPS)):
        q_bd = jnp.where(bd, q_ref[g], 0.0)
        k_old = s_ref[0].reshape(rw, window)
        v_old = s_ref[1].reshape(rw, window)
        s = jnp.dot(q_bd.astype(BF16), k_old.astype(BF16), preferred_element_type=F32)
        pos = lax.broadcasted_iota(jnp.int32, (8, window), 1)
        back = (window - pos).astype(F32)
        s = jnp.where((pos & (d - 1)) == 0, s - slope * back, NEG_INF)
        s_new = jnp.sum(q_bd * kr_ref[g], axis=1, keepdims=True)
        m = jnp.maximum(jnp.max(s, axis=1, keepdims=True), s_new)
        e = jnp.exp(s - m)
        e_new = jnp.exp(s_new - m)
        den = jnp.sum(e, axis=1, keepdims=True) + e_new
        o = (lax.dot_general(e.astype(BF16), v_old.astype(BF16), NT, preferred_element_type=F32)
             + e_new * jnp.where(bd, vr_ref[g], 0.0))
        parts.append((o, m, den))
        last = lax.broadcasted_iota(jnp.int32, (rw, window), 1) == window - 1
        for t, (old, c_ref) in enumerate(((k_old, kc_ref), (v_old, vc_ref))):
            col = jnp.sum(jnp.where(is_b, c_ref[g], 0.0), axis=1, keepdims=True)
            new = jnp.where(last, col, pltpu.roll(old, window - 1, 1))
            n_ref[t] = new.reshape(DIL_HEAD_BLOCK, DIL_HEAD_DIM, window)
    m_all = jnp.maximum(jnp.maximum(parts[0][1], parts[1][1]), parts[2][1])
    num = jnp.zeros((8, rw), F32)
    den = jnp.zeros((8, 1), F32)
    for o, m, dn in parts:
        wgt = jnp.exp(m - m_all)
        num = num + o * wgt
        den = den + dn * wgt
    o_ref[...] = jnp.sum(jnp.where(bd, num / den, 0.0), axis=0, keepdims=True).astype(o_ref.dtype)


def _dil_sample(q_s, kv_s, kvt_s, states_t, layer, prev):
    ng = len(DIL_GROUPS)
    w = DIL_WIDTH
    hbk = DIL_HEAD_BLOCK
    rw = hbk * DIL_HEAD_DIM
    nhb = DIL_HEADS // hbk
    row = lambda off: pl.BlockSpec((ng, None, 1, rw), lambda b, hb, off=off: (0, b, 0, off + hb))
    colspec = lambda off: pl.BlockSpec((ng, rw, LANES), lambda b, hb, off=off: (0, off + hb, 0))
    st_spec = lambda window: pl.BlockSpec((None, None, 2, hbk, DIL_HEAD_DIM, window),
                                          lambda b, hb: (layer, b, 0, hb, 0, 0))
    st_specs = [st_spec(window) for window, _ in DIL_GROUPS]
    in_specs = [row(0), row(0), row(nhb), colspec(0), colspec(nhb)] + st_specs
    args = [q_s.reshape(ng, DEC_BATCH, 1, w), kv_s.reshape(ng, DEC_BATCH, 1, 2 * w),
            kv_s.reshape(ng, DEC_BATCH, 1, 2 * w), kvt_s, kvt_s, *states_t]
    aliases = {len(args) + k: 1 + k for k in range(ng)}
    in_specs += [_any_spec()] * ng
    args += list(prev)
    return pl.pallas_call(
        _dil_sample_kernel, grid=(DEC_BATCH, nhb),
        in_specs=in_specs,
        out_specs=[pl.BlockSpec((None, 1, rw), lambda b, hb: (b, 0, hb))] + st_specs,
        out_shape=[jax.ShapeDtypeStruct((DEC_BATCH, 1, w), BF16)]
                  + [jax.ShapeDtypeStruct(st.shape, F32) for st in states_t],
        input_output_aliases=aliases,
        compiler_params=_cparams(("parallel", "parallel")), name="dil_attn_sample",
    )(*args)


def _rope_tables(pos):
    half = MLA_ROPE // 2
    inv = ROPE_THETA ** (-jnp.arange(half, dtype=F32) / half)
    ang = pos.astype(F32)[:, None] * inv[None, :]
    cos, sin = jnp.cos(ang), jnp.sin(ang)
    n = pos.shape[0]
    scale = (MLA_NOPE + MLA_ROPE) ** -0.5 * LOG2E
    zq = jnp.zeros((n, LANES - MLA_NOPE - MLA_ROPE), F32)
    cq = jnp.concatenate([jnp.ones((n, MLA_NOPE), F32), cos, cos, zq], axis=1) * scale
    sq = jnp.concatenate([jnp.zeros((n, MLA_NOPE), F32), -sin, sin, zq], axis=1) * scale
    zk = jnp.zeros((n, LANES - MLA_ROPE), F32)
    ck = jnp.concatenate([cos, cos, zk], axis=1)
    sk = jnp.concatenate([-sin, sin, zk], axis=1)
    return cq, sq, ck, sk


def _mla_weights(w_dq, w_uq, w_dkv, w_uk, w_uv):
    half = MLA_ROPE // 2
    padl = lambda a, wdt: jnp.pad(a, [(0, 0)] * (a.ndim - 1) + [(0, wdt - a.shape[-1])])
    c, r = w_dkv[:, :MLA_KV_LORA], w_dkv[:, MLA_KV_LORA:]
    r_sw = jnp.concatenate([r[:, half:], r[:, :half]], axis=1)
    wa = jnp.concatenate([w_dq, c, padl(r, LANES), padl(r_sw, LANES)], axis=1).astype(BF16)
    uq = w_uq.reshape(MLA_Q_LORA, MLA_HEADS, MLA_NOPE + MLA_ROPE)
    wq1 = padl(uq, LANES).reshape(MLA_Q_LORA, HEAD_PAD).astype(BF16)
    uq_sw = jnp.concatenate([jnp.zeros_like(uq[..., :MLA_NOPE]), uq[..., MLA_NOPE + half:],
                             uq[..., MLA_NOPE:MLA_NOPE + half]], axis=-1)
    wq2 = padl(uq_sw, LANES).reshape(MLA_Q_LORA, HEAD_PAD).astype(BF16)
    wk_top = padl(w_uk, LANES).reshape(MLA_KV_LORA, HEAD_PAD)
    place = jnp.pad(jnp.eye(MLA_ROPE, dtype=F32), ((0, LANES - MLA_ROPE), (MLA_NOPE, LANES - MLA_NOPE - MLA_ROPE)))
    wk = jnp.concatenate([wk_top, jnp.tile(place, (1, MLA_HEADS))], axis=0).astype(BF16)
    wv = w_uv.reshape(MLA_KV_LORA, MLA_HEADS * MLA_VDIM).astype(BF16)
    w_qlat = padl(jnp.transpose(w_uk, (1, 0, 2)), LANES)
    w_qlat = jnp.transpose(w_qlat, (0, 2, 1)).astype(BF16)
    uv = jnp.transpose(w_uv, (1, 0, 2))
    z = jnp.zeros_like(uv)
    even = jnp.concatenate([uv, z], axis=-1)
    odd = jnp.concatenate([z, uv], axis=-1)
    is_even = (jnp.arange(MLA_HEADS) % 2 == 0)[:, None, None]
    w_vpair = jnp.where(is_even, even, odd).astype(BF16)
    return wa, wq1, wq2, wk, wv, w_qlat, w_vpair


def kernel(x_prompt, x_sample, cache_mla_ckv, cache_mla_kpe, page_table, state_dil_kv_w128, state_dil_kv_w512, state_dil_kv_w2048, mla_w_dq, mla_g_q, mla_w_uq, mla_w_dkv, mla_g_kv, mla_w_uk, mla_w_uv, mla_w_o, dil_w_qkv, dil_w_o, norm_mix, norm_ffn, ffn_w_up, ffn_w_down, norm_final):
    tm = 512
    xp = x_prompt.reshape(N_PROMPT, D_MODEL)
    xs = x_sample.reshape(DEC_BATCH, D_MODEL)
    tabs_p = _rope_tables(jnp.arange(SEQ))
    tabs_s = _rope_tables(jnp.full((DEC_BATCH,), PAST_LEN))
    gfin = norm_final.reshape(1, D_MODEL)
    wup = ffn_w_up.astype(BF16)
    wdn = ffn_w_down.astype(BF16)
    kpe_pool_t = jnp.transpose(cache_mla_kpe, (0, 1, 3, 2))
    states_t = [jnp.transpose(st, (0, 1, 3, 4, 5, 2))
                for st in (state_dil_kv_w128, state_dil_kv_w512, state_dil_kv_w2048)]
    ckv_p, kpe_p, ckv_s, kpe_s = [], [], [], []
    for layer in range(DEPTH):
        i = layer // 2
        gmix = norm_mix[layer].reshape(1, D_MODEL)
        if layer % 2 == 0:
            wa, wq1, wq2, wk, wv, w_qlat, w_vpair = _mla_weights(mla_w_dq[i], mla_w_uq[i], mla_w_dkv[i], mla_w_uk[i], mla_w_uv[i])
            gq = mla_g_q[i].reshape(1, MLA_Q_LORA)
            gkv = mla_g_kv[i].reshape(1, MLA_KV_LORA)
            q_p, c_p, k_p, kf_p, v_p = _mla_proj(xp, gmix, wa, gq, wq1, wq2, gkv, tabs_p, wk, wv, tm=tm, with_kv=True)
            q_s, c_s, k_s = _mla_proj(xs, gmix, wa, gq, wq1, wq2, gkv, tabs_s, None, None, tm=DEC_BATCH, with_kv=False)
            qlat_t = jnp.transpose(_mla_qlat(q_s, w_qlat), (1, 2, 0))
            qpe_t = jnp.transpose(q_s.reshape(DEC_BATCH, MLA_HEADS, LANES)[:, :, MLA_NOPE:MLA_NOPE + MLA_ROPE],
                                  (0, 2, 1)).astype(F32)
            if layer == 0:
                o_p, *fills = _mla_attn(q_p, kf_p, v_p, tq=512, fill=True)
                dil_p_bufs, dil_s_bufs = fills[:len(DIL_GROUPS)], fills[len(DIL_GROUPS):]
                olat = _mla_sample_attn(page_table, qlat_t, qpe_t, c_s.T, k_s.T, cache_mla_ckv, kpe_pool_t, i)
            else:
                o_p, olat = _mla_fused_attn(q_p, kf_p, v_p, page_table, qlat_t, qpe_t, c_s.T, k_s.T,
                                            cache_mla_ckv, kpe_pool_t, i, tq=512)
            o_s = _mla_sample_out(jnp.transpose(olat, (1, 0, 2)), w_vpair)
            wo = mla_w_o[i].astype(BF16)
            ckv_p.append(c_p.reshape(BATCH, SEQ, MLA_KV_LORA))
            kpe_p.append(k_p)
            ckv_s.append(c_s.reshape(DEC_BATCH, 1, MLA_KV_LORA))
            kpe_s.append(k_s.reshape(DEC_BATCH, 1, MLA_ROPE))
        else:
            wqkv = dil_w_qkv[i].astype(BF16)
            q_p, kv_p, *dil_p_bufs = _dil_proj_prompt(xp, gmix, wqkv, i, dil_p_bufs, tm=256)
            q_s, kv_s, kvt_s = _dil_proj_sample(xs, gmix, wqkv)
            o_p = _dil_attn(q_p, kv_p)
            o_s, *dil_s_bufs = _dil_sample(q_s, kv_s, kvt_s, states_t, i, dil_s_bufs)
            o_s = o_s.reshape(DEC_BATCH, DIL_WIDTH)
            wo = dil_w_o[i].astype(BF16)
        final = layer == DEPTH - 1
        g_ffn = norm_ffn[layer].reshape(1, D_MODEL)
        xp = _post_ffn(xp, o_p, wo, g_ffn, wup, wdn, gfin, layer, tm=tm, final=final)
        xs = _post_ffn(xs, o_s, wo, g_ffn, wup, wdn, gfin, layer, tm=DEC_BATCH, final=final)
    nl = DEPTH // 2
    to_window = lambda a, nb: jnp.transpose(
        a.reshape(nl, nb, 2, DIL_HEADS, DIL_HEAD_DIM, a.shape[-1]), (0, 1, 5, 2, 3, 4))
    return (xp.reshape(BATCH, SEQ, D_MODEL), xs.reshape(DEC_BATCH, 1, D_MODEL),
            jnp.stack(ckv_p), jnp.transpose(jnp.stack(kpe_p), (0, 1, 3, 2)), jnp.stack(ckv_s), jnp.stack(kpe_s),
            *[to_window(a, BATCH) for a in dil_p_bufs],
            *[to_window(a, DEC_BATCH) for a in dil_s_bufs])
```

```python
import functools
import math

import jax
import jax.numpy as jnp
from jax import lax
from jax.experimental import pallas as pl
from jax.experimental.pallas import tpu as pltpu

F32 = jnp.float32
BF16 = jnp.bfloat16

D_MODEL = 1024
BATCH = 8
SEQ = 2048
DEPTH = 4
DEC_BATCH = 32
PAST_LEN = 16384
PAGE_SIZE = 128
N_PAGES = PAST_LEN // PAGE_SIZE
MLA_HEADS = 16
MLA_NOPE = 64
MLA_ROPE = 32
MLA_VDIM = 64
MLA_Q_LORA = 384
MLA_KV_LORA = 256
ROPE_THETA = 10000.0
DIL_GROUPS = ((128, 1), (512, 4), (2048, 16))
DIL_SPAN = 128
DIL_HEADS = 8
DIL_HEAD_DIM = 64
DIL_WIDTH = DIL_HEADS * DIL_HEAD_DIM
D_FF = 4 * D_MODEL
EPS = 1e-6
NEG_INF = -1e30
LOG2E = math.log2(math.e)

LANES = 128
N_PROMPT = BATCH * SEQ
HEAD_PAD = MLA_HEADS * LANES
VMEM_LIMIT = 56 * 1024 * 1024
FUSED_PAGES_PER_STEP = 64
PAGES_PER_CHUNK = 2
FILL_COPIES = len(DIL_GROUPS) * (DEPTH // 2) * (2 + DEC_BATCH // BATCH)
DIL_HEAD_BLOCK = 4
NT = (((1,), (1,)), ((), ()))
TN = (((0,), (0,)), ((), ()))


def _cparams(sem):
    return pltpu.CompilerParams(dimension_semantics=sem, vmem_limit_bytes=VMEM_LIMIT)


def _rms(x, g):
    return x * lax.rsqrt(jnp.mean(x * x, axis=-1, keepdims=True) + EPS) * g


def _const_spec(shape):
    nd = len(shape)
    return pl.BlockSpec(shape, lambda *_: (0,) * nd, pipeline_mode=pl.Buffered(1))


def _any_spec():
    return pl.BlockSpec(memory_space=pl.ANY)


def _mla_proj_kernel(*refs, with_kv):
    if with_kv:
        (x_ref, gmix_ref, wa_ref, gq_ref, wq1_ref, wq2_ref, gkv_ref, cq_ref, sq_ref, ck_ref, sk_ref,
         wk_ref, wv_ref, q_out, ckv_out, kpe_out, k_out, v_out) = refs
    else:
        (x_ref, gmix_ref, wa_ref, gq_ref, wq1_ref, wq2_ref, gkv_ref, cq_ref, sq_ref, ck_ref, sk_ref,
         q_out, ckv_out, kpe_out) = refs
    h = _rms(x_ref[...], gmix_ref[...]).astype(BF16)
    a = jnp.dot(h, wa_ref[...], preferred_element_type=F32)
    c_q = _rms(a[:, :MLA_Q_LORA], gq_ref[...]).astype(BF16)
    o1 = MLA_Q_LORA + MLA_KV_LORA
    c_kv = _rms(a[:, MLA_Q_LORA:o1], gkv_ref[...])
    k_pe = a[:, o1:o1 + LANES] * ck_ref[...] + a[:, o1 + LANES:o1 + 2 * LANES] * sk_ref[...]
    ckv_out[...] = c_kv
    if with_kv:
        kpe_out[...] = k_pe.T[:MLA_ROPE, :]
    else:
        kpe_out[...] = k_pe[:, :MLA_ROPE]
    heads_per_chunk = 4
    cw = heads_per_chunk * LANES
    cq_c = jnp.concatenate([cq_ref[...]] * heads_per_chunk, axis=1)
    sq_c = jnp.concatenate([sq_ref[...]] * heads_per_chunk, axis=1)
    for c in range(MLA_HEADS // heads_per_chunk):
        q1 = jnp.dot(c_q, wq1_ref[:, c * cw:(c + 1) * cw], preferred_element_type=F32)
        q2 = jnp.dot(c_q, wq2_ref[:, c * cw:(c + 1) * cw], preferred_element_type=F32)
        q_out[:, c * cw:(c + 1) * cw] = (q1 * cq_c + q2 * sq_c).astype(q_out.dtype)
    if with_kv:
        c_kv_b = c_kv.astype(BF16)
        lhs = jnp.concatenate([c_kv_b, k_pe.astype(BF16)], axis=1)
        k_out[...] = jnp.dot(lhs, wk_ref[...], preferred_element_type=F32).astype(k_out.dtype)
        v_out[...] = jnp.dot(c_kv_b, wv_ref[...], preferred_element_type=F32).astype(v_out.dtype)


def _mla_proj(x, gmix, wa, gq, wq1, wq2, gkv, tabs, wk, wv, *, tm, with_kv):
    n = x.shape[0]
    nt = n // tm
    tab_blocks = tabs[0].shape[0] // tm
    row = lambda w: pl.BlockSpec((tm, w), lambda i: (i, 0))
    tab = pl.BlockSpec((tm, LANES), lambda i: (i % tab_blocks, 0))
    in_specs = [row(D_MODEL), _const_spec(gmix.shape), _const_spec(wa.shape), _const_spec(gq.shape),
                _const_spec(wq1.shape), _const_spec(wq2.shape), _const_spec(gkv.shape), tab, tab, tab, tab]
    args = [x, gmix, wa, gq, wq1, wq2, gkv, *tabs]
    out_shape = [jax.ShapeDtypeStruct((n, HEAD_PAD), BF16), jax.ShapeDtypeStruct((n, MLA_KV_LORA), F32)]
    out_specs = [row(HEAD_PAD), row(MLA_KV_LORA)]
    if with_kv:
        per_b = SEQ // tm
        in_specs += [_const_spec(wk.shape), _const_spec(wv.shape)]
        args += [wk, wv]
        out_shape += [jax.ShapeDtypeStruct((BATCH, MLA_ROPE, SEQ), F32),
                      jax.ShapeDtypeStruct((n, HEAD_PAD), BF16),
                      jax.ShapeDtypeStruct((n, MLA_HEADS * MLA_VDIM), BF16)]
        out_specs += [pl.BlockSpec((None, MLA_ROPE, tm), lambda i: (i // per_b, 0, i % per_b)),
                      row(HEAD_PAD), row(MLA_HEADS * MLA_VDIM)]
    else:
        out_shape.append(jax.ShapeDtypeStruct((n, MLA_ROPE), F32))
        out_specs.append(row(MLA_ROPE))
    return pl.pallas_call(
        functools.partial(_mla_proj_kernel, with_kv=with_kv),
        grid=(nt,), in_specs=in_specs, out_specs=out_specs, out_shape=out_shape,
        compiler_params=_cparams(("parallel",)), name="mla_proj_kv" if with_kv else "mla_proj_q",
    )(*args)


def _prompt_attn_body(q_ref, k_ref, v_ref, o_ref, tq):
    row = lax.broadcasted_iota(jnp.int32, (tq, tq), 0)
    col = lax.broadcasted_iota(jnp.int32, (tq, tq), 1)
    causal = col <= row
    lane = lax.broadcasted_iota(jnp.int32, (tq, LANES), 1)
    for i in reversed(range(SEQ // tq)):
        w0 = i * tq
        outs = []
        for hh in range(2):
            hs = slice(hh * LANES, (hh + 1) * LANES)
            q = q_ref[w0:w0 + tq, hs]
            s_d = lax.dot_general(q, k_ref[w0:w0 + tq, hs], NT, preferred_element_type=F32)
            s_d = jnp.where(causal, s_d, NEG_INF)
            m = jnp.max(s_d, axis=-1, keepdims=True)
            if i > 0:
                s_o = lax.dot_general(q, k_ref[0:w0, hs], NT, preferred_element_type=F32)
                m = jnp.maximum(m, jnp.max(s_o, axis=-1, keepdims=True))
            p_d = jnp.exp2(s_d - m)
            l = jnp.sum(p_d, axis=-1, keepdims=True)
            acc = jnp.dot(p_d.astype(BF16), v_ref[w0:w0 + tq, :], preferred_element_type=F32)
            if i > 0:
                p_o = jnp.exp2(s_o - m)
                l = l + jnp.sum(p_o, axis=-1, keepdims=True)
                acc = acc + jnp.dot(p_o.astype(BF16), v_ref[0:w0, :], preferred_element_type=F32)
            outs.append(acc / l)
        o_ref[w0:w0 + tq, :] = jnp.where(lane < MLA_VDIM, outs[0], outs[1]).astype(o_ref.dtype)


def _window_fill_shapes():
    nl = DEPTH // 2
    prompt = [jax.ShapeDtypeStruct((nl, BATCH, 2 * DIL_WIDTH, window), F32) for window, _ in DIL_GROUPS]
    sample = [jax.ShapeDtypeStruct((nl, DEC_BATCH, 2, DIL_HEADS, DIL_HEAD_DIM, window), F32)
              for window, _ in DIL_GROUPS]
    return prompt + sample


def _mla_qlat_kernel(q_ref, w_ref, o_ref):
    for h in range(MLA_HEADS):
        o_ref[h] = jnp.dot(q_ref[:, h * LANES:(h + 1) * LANES], w_ref[h], preferred_element_type=F32)


def _mla_qlat(q_s, w_qlat):
    return pl.pallas_call(
        _mla_qlat_kernel,
        out_shape=jax.ShapeDtypeStruct((MLA_HEADS, DEC_BATCH, MLA_KV_LORA), F32),
        compiler_params=pltpu.CompilerParams(vmem_limit_bytes=VMEM_LIMIT), name="mla_qlat",
    )(q_s, w_qlat)


def _sample_attn_init(batch, qlat_ref, qpe_ref, cnew_ref, knew_ref, cnew_row_ref, m_s, l_s, acc_s):
    pick = lambda ref: jnp.sum(
        jnp.where(lax.broadcasted_iota(jnp.int32, ref.shape, 1) == batch, ref[...], 0.0), axis=1, keepdims=True)
    s_new = (jnp.sum(qlat_ref[...] * pick(cnew_ref), axis=0, keepdims=True)
             + jnp.sum(qpe_ref[...] * pick(knew_ref), axis=0, keepdims=True))
    m_s[...] = s_new
    l_s[...] = jnp.ones_like(s_new)
    acc_s[...] = jnp.broadcast_to(cnew_row_ref[...], acc_s.shape)


def _heads_to_col(r):
    eye = (lax.broadcasted_iota(jnp.int32, (MLA_HEADS, MLA_HEADS), 0)
           == lax.broadcasted_iota(jnp.int32, (MLA_HEADS, MLA_HEADS), 1))
    return jnp.sum(jnp.where(eye, r, 0.0), axis=1, keepdims=True)


def _sample_attn_pages(qlat_ref, qpe_ref, ckv_refs, kpe_refs, m_s, l_s, acc_s, per):
    qlat_b = qlat_ref[...].astype(BF16)
    qpe_b = qpe_ref[...].astype(BF16)
    m_old = m_s[...]
    m_new = m_old
    parts = []
    for j in range(len(ckv_refs) // per):
        ckv = jnp.concatenate([r[...] for r in ckv_refs[j * per:(j + 1) * per]], axis=0).astype(BF16)
        kpe_t = jnp.concatenate([r[...] for r in kpe_refs[j * per:(j + 1) * per]], axis=1).astype(BF16)
        s = (jnp.dot(ckv, qlat_b, preferred_element_type=F32)
             + lax.dot_general(kpe_t, qpe_b, TN, preferred_element_type=F32))
        m_new = jnp.maximum(m_new, jnp.max(s, axis=0, keepdims=True))
        parts.append((s, ckv))
    alpha = jnp.exp2(m_old - m_new)
    l = alpha * l_s[...]
    acc = _heads_to_col(alpha) * acc_s[...]
    for s, ckv in parts:
        p = jnp.exp2(s - m_new)
        l = l + jnp.sum(p, axis=0, keepdims=True)
        acc = acc + lax.dot_general(p.astype(BF16), ckv, TN, preferred_element_type=F32)
    m_s[...] = m_new
    l_s[...] = l
    acc_s[...] = acc
    return l, acc


def _mla_fused_attn_kernel(pt_ref, q_ref, k_ref, v_ref, qlat_ref, qpe_ref, cnew_ref, knew_ref, cnew_row_ref,
                           *rest, tq, npg, fill):
    del pt_ref
    ckv_refs = rest[:npg]
    kpe_refs = rest[npg:2 * npg]
    rest = rest[2 * npg:]
    b = pl.program_id(0)
    j = pl.program_id(1)
    t = b * pl.num_programs(1) + j
    steps_per_row = N_PAGES // npg
    copies = []
    if fill:
        o_ref, olat_ref, *bufs, m_s, l_s, acc_s, zero_s, sems = rest

        @pl.when(t == 0)
        def _():
            zero_s[...] = jnp.zeros(zero_s.shape, zero_s.dtype)

        ng = len(DIL_GROUPS)
        per_b = DEC_BATCH // BATCH
        for g, (window, _) in enumerate(DIL_GROUPS):
            cols = pl.ds(0, window)
            for layer in range(DEPTH // 2):
                for half in range(2):
                    rows = pl.ds(j * (2 * DIL_HEAD_DIM) + half * DIL_HEAD_DIM, DIL_HEAD_DIM)
                    copies.append((zero_s.at[0, :, cols], bufs[g].at[layer, b, rows, :]))
                for bb in range(per_b):
                    copies.append((zero_s.at[:, :, cols], bufs[ng + g].at[layer, b * per_b + bb, :, j]))
        copies = [pltpu.make_async_copy(src, dst, sems.at[n]) for n, (src, dst) in enumerate(copies)]
        assert len(copies) == FILL_COPIES
        for c in copies:
            c.start()
    else:
        o_ref, olat_ref, m_s, l_s, acc_s = rest

    @pl.when(t % steps_per_row == 0)
    def _():
        _sample_attn_init(t // steps_per_row, qlat_ref, qpe_ref, cnew_ref, knew_ref, cnew_row_ref, m_s, l_s, acc_s)

    l, acc = _sample_attn_pages(qlat_ref, qpe_ref, ckv_refs, kpe_refs, m_s, l_s, acc_s, PAGES_PER_CHUNK)
    _prompt_attn_body(q_ref, k_ref, v_ref, o_ref, tq)
    for c in copies:
        c.wait()

    @pl.when(t % steps_per_row == steps_per_row - 1)
    def _():
        olat_ref[...] = acc / _heads_to_col(l)


def _mla_fused_attn(q, k, v, page_table, qlat_t, qpe_t, cnew_t, knew_t, ckv_pool, kpe_pool_t, layer, *, tq, fill):
    pairs = MLA_HEADS // 2
    npg = FUSED_PAGES_PER_STEP
    steps_per_row = N_PAGES // npg
    assert BATCH * pairs * npg == DEC_BATCH * N_PAGES and pairs == DIL_HEADS
    fill_shapes = _window_fill_shapes() if fill else []
    fill_scratch = [pltpu.VMEM((2, DIL_HEAD_DIM, max(w for w, _ in DIL_GROUPS)), F32),
                    pltpu.SemaphoreType.DMA((FILL_COPIES,))] if fill else []
    srow = lambda b, j: (b * pairs + j) // steps_per_row
    prompt = lambda w: pl.BlockSpec((SEQ, w), lambda b, j, pt: (b, j))
    per_row = lambda r, w: pl.BlockSpec((None, r, w), lambda b, j, pt: (srow(b, j), 0, 0))
    whole = lambda r: pl.BlockSpec((r, DEC_BATCH), lambda b, j, pt: (0, 0))
    page = lambda r, w, kk: pl.BlockSpec(
        (None, None, r, w),
        lambda b, j, pt, kk=kk: (layer, pt[srow(b, j), ((b * pairs + j) % steps_per_row) * npg + kk], 0, 0))
    in_specs = ([prompt(2 * LANES), prompt(2 * LANES), prompt(LANES),
                 per_row(MLA_KV_LORA, MLA_HEADS), per_row(MLA_ROPE, MLA_HEADS), whole(MLA_KV_LORA), whole(MLA_ROPE),
                 per_row(1, MLA_KV_LORA)]
                + [page(PAGE_SIZE, MLA_KV_LORA, kk) for kk in range(npg)]
                + [page(MLA_ROPE, PAGE_SIZE, kk) for kk in range(npg)])
    grid_spec = pltpu.PrefetchScalarGridSpec(
        num_scalar_prefetch=1, grid=(BATCH, pairs), in_specs=in_specs,
        out_specs=[pl.BlockSpec((SEQ, LANES), lambda b, j, pt: (b, j)),
                   pl.BlockSpec((None, MLA_HEADS, MLA_KV_LORA), lambda b, j, pt: (srow(b, j), 0, 0))]
                  + [_any_spec()] * len(fill_shapes),
        scratch_shapes=[pltpu.VMEM((1, MLA_HEADS), F32), pltpu.VMEM((1, MLA_HEADS), F32),
                        pltpu.VMEM((MLA_HEADS, MLA_KV_LORA), F32)] + fill_scratch)
    return pl.pallas_call(
        functools.partial(_mla_fused_attn_kernel, tq=tq, npg=npg, fill=fill), grid_spec=grid_spec,
        out_shape=[jax.ShapeDtypeStruct((N_PROMPT, MLA_HEADS * MLA_VDIM), BF16),
                   jax.ShapeDtypeStruct((DEC_BATCH, MLA_HEADS, MLA_KV_LORA), F32)] + fill_shapes,
        compiler_params=_cparams(("arbitrary", "arbitrary")), name="mla_attn_fused",
    )(page_table, q, k, v, qlat_t, qpe_t, cnew_t, knew_t, cnew_t.T.reshape(DEC_BATCH, 1, MLA_KV_LORA),
      *([ckv_pool] * npg), *([kpe_pool_t] * npg))


def _mla_sample_out_kernel(olat_ref, w_ref, o_ref):
    for j in range(MLA_HEADS // 2):
        acc = None
        for h in (2 * j, 2 * j + 1):
            t = jnp.dot(olat_ref[h].astype(BF16), w_ref[h], preferred_element_type=F32)
            acc = t if acc is None else acc + t
        o_ref[:, j * LANES:(j + 1) * LANES] = acc.astype(o_ref.dtype)


def _mla_sample_out(olat_t, w_vpair):
    return pl.pallas_call(
        _mla_sample_out_kernel,
        out_shape=jax.ShapeDtypeStruct((DEC_BATCH, MLA_HEADS * MLA_VDIM), BF16),
        compiler_params=pltpu.CompilerParams(vmem_limit_bytes=VMEM_LIMIT), name="mla_sample_out",
    )(olat_t, w_vpair)


def _post_ffn_kernel(*refs, final, ff_chunk):
    if final:
        x_ref, o_ref, wo_ref, g_ref, wup_ref, wdn_ref, gfin_ref, out_ref = refs
    else:
        x_ref, o_ref, wo_ref, g_ref, wup_ref, wdn_ref, out_ref = refs
    x1 = x_ref[...] + jnp.dot(o_ref[...], wo_ref[...], preferred_element_type=F32)
    hn = _rms(x1, g_ref[...]).astype(BF16)
    acc = x1
    for c in range(D_FF // ff_chunk):
        u = jnp.dot(hn, wup_ref[:, c * ff_chunk:(c + 1) * ff_chunk], preferred_element_type=F32)
        u = jnp.square(jnp.maximum(u, 0.0)).astype(BF16)
        acc = acc + jnp.dot(u, wdn_ref[c * ff_chunk:(c + 1) * ff_chunk, :], preferred_element_type=F32)
    if final:
        acc = _rms(acc, gfin_ref[...])
    out_ref[...] = acc


def _post_ffn(x, o, wo, g, wup, wdn, gfin, layer, *, tm, final):
    n = x.shape[0]
    ko = o.shape[1]
    row = lambda w: pl.BlockSpec((tm, w), lambda i: (i, 0))
    per_layer = lambda a: pl.BlockSpec((None,) + a.shape[1:], lambda i: (layer, 0, 0), pipeline_mode=pl.Buffered(1))
    in_specs = [row(D_MODEL), row(ko), _const_spec(wo.shape), _const_spec(g.shape),
                per_layer(wup), per_layer(wdn)]
    args = [x, o, wo, g, wup, wdn]
    if final:
        in_specs.append(_const_spec(gfin.shape))
        args.append(gfin)
    return pl.pallas_call(
        functools.partial(_post_ffn_kernel, final=final, ff_chunk=1024),
        grid=(n // tm,), in_specs=in_specs, out_specs=row(D_MODEL),
        out_shape=jax.ShapeDtypeStruct((n, D_MODEL), F32),
        compiler_params=_cparams(("parallel",)), name="post_ffn_final" if final else "post_ffn",
    )(*args)


def _dil_proj_prompt_kernel(*refs):
    x_ref, g_ref, w_ref = refs[:3]
    q_out, kv_out, b128_out, b512_out, b2048_out = refs[-5:]
    h = _rms(x_ref[...], g_ref[...]).astype(BF16)
    gw = 3 * DIL_WIDTH
    tm = x_ref.shape[0]
    for g, buf_out in enumerate((b128_out, b512_out, b2048_out)):
        r = jnp.dot(h, w_ref[:, g * gw:(g + 1) * gw], preferred_element_type=F32)
        q_out[g] = r[:, :DIL_WIDTH] * (DIL_HEAD_DIM ** -0.5)
        kv = r[:, DIL_WIDTH:]
        kv_out[g] = kv
        kv_t = kv.T
        wb = buf_out.shape[-1]
        buf_out[...] = kv_t[:, tm - wb:] if wb < tm else kv_t


def _dil_proj_prompt(x, g, w, layer, prev, *, tm):
    n = x.shape[0]
    ng = len(DIL_GROUPS)
    per_b = SEQ // tm
    kvw = 2 * DIL_WIDTH
    buf_shapes, buf_specs = [], []
    for window, _ in DIL_GROUPS:
        wb = min(window, tm)
        nblk = window // wb
        first = per_b - nblk
        buf_shapes.append(jax.ShapeDtypeStruct((DEPTH // 2, BATCH, kvw, window), F32))
        buf_specs.append(pl.BlockSpec(
            (None, None, kvw, wb),
            lambda i, first=first: (layer, i // per_b, 0, jnp.maximum(i % per_b - first, 0))))
    in_specs = [pl.BlockSpec((tm, D_MODEL), lambda i: (i, 0)), _const_spec(g.shape), _const_spec(w.shape)]
    in_specs += [_any_spec()] * ng
    args = [x, g, w, *prev]
    aliases = {3 + k: 2 + k for k in range(ng)}
    return pl.pallas_call(
        _dil_proj_prompt_kernel, grid=(n // tm,),
        in_specs=in_specs,
        out_specs=[pl.BlockSpec((ng, tm, DIL_WIDTH), lambda i: (0, i, 0)),
                   pl.BlockSpec((ng, tm, kvw), lambda i: (0, i, 0))] + buf_specs,
        out_shape=[jax.ShapeDtypeStruct((ng, n, DIL_WIDTH), F32), jax.ShapeDtypeStruct((ng, n, kvw), F32)] + buf_shapes,
        input_output_aliases=aliases,
        compiler_params=_cparams(("arbitrary",)), name="dil_proj_prompt",
    )(*args)


def _dil_proj_sample_kernel(x_ref, g_ref, w_ref, q_out, kv_out, kvt_out):
    h = _rms(x_ref[...], g_ref[...]).astype(BF16)
    gw = 3 * DIL_WIDTH
    pad = jnp.zeros((LANES - DEC_BATCH, 2 * DIL_WIDTH), F32)
    for g in range(len(DIL_GROUPS)):
        r = jnp.dot(h, w_ref[:, g * gw:(g + 1) * gw], preferred_element_type=F32)
        q_out[g] = r[:, :DIL_WIDTH] * (DIL_HEAD_DIM ** -0.5)
        kv = r[:, DIL_WIDTH:]
        kv_out[g] = kv
        kvt_out[g] = jnp.concatenate([kv, pad], axis=0).T


def _dil_proj_sample(x, g, w):
    ng = len(DIL_GROUPS)
    return pl.pallas_call(
        _dil_proj_sample_kernel,
        out_shape=[jax.ShapeDtypeStruct((ng, DEC_BATCH, DIL_WIDTH), F32),
                   jax.ShapeDtypeStruct((ng, DEC_BATCH, 2 * DIL_WIDTH), F32),
                   jax.ShapeDtypeStruct((ng, 2 * DIL_WIDTH, LANES), F32)],
        compiler_params=pltpu.CompilerParams(vmem_limit_bytes=VMEM_LIMIT), name="dil_proj_sample",
    )(x, g, w)


def _strided(start, size, stride):
    return pl.ds(start, size) if stride == 1 else pl.ds(start, size, stride=stride)


def _dil_attn_kernel(q_ref, k_ref, v_ref, o_ref, os_ref, ms_ref, ds_ref):
    j = pl.program_id(1)
    sp = DIL_SPAN
    lane = lax.broadcasted_iota(jnp.int32, (sp, LANES), 1)
    first_head = lane < DIL_HEAD_DIM
    jq = lax.broadcasted_iota(jnp.int32, (sp, 2 * sp), 0)
    jk = lax.broadcasted_iota(jnp.int32, (sp, 2 * sp), 1)
    dist = jq + sp - jk
    valid = (dist >= 0) & (dist <= sp)
    slopes = [jnp.exp2(-(8.0 / DIL_HEADS) * (2 * j + hh + 1).astype(F32)) for hh in range(2)]
    sel = lambda a: jnp.where(first_head, a[:sp], a[sp:])
    wide = lambda a: jnp.broadcast_to(a, (2 * sp, LANES))
    own_lanes = ((lax.broadcasted_iota(jnp.int32, (2 * sp, LANES), 1) < DIL_HEAD_DIM)
                 == (lax.broadcasted_iota(jnp.int32, (2 * sp, LANES), 0) < sp))
    for g, (_, d) in enumerate(DIL_GROUPS):
        dist_f = (dist * d).astype(F32)
        bias_full = jnp.concatenate([jnp.where(valid, -slopes[hh] * dist_f, NEG_INF) for hh in range(2)], axis=0)
        bias_cur = bias_full[:, sp:]
        for n in range(SEQ // (sp * d)):
            for r in range(d):
                qs = n * sp * d + r
                rows = _strided(qs, sp, d)
                q = q_ref[g, rows, :]
                k = k_ref[g, rows, :]
                v = v_ref[g, rows, :]
                if n > 0:
                    prev = _strided(qs - sp * d, sp, d)
                    k = jnp.concatenate([k_ref[g, prev, :], k], axis=0)
                    v = jnp.concatenate([v_ref[g, prev, :], v], axis=0)
                q2 = jnp.where(own_lanes, jnp.concatenate([q, q], axis=0), 0.0).astype(BF16)
                s = lax.dot_general(q2, k.astype(BF16), NT, preferred_element_type=F32)
                s = s + (bias_full if n > 0 else bias_cur)
                m = jnp.max(s, axis=-1, keepdims=True)
                e = jnp.exp(s - m)
                den = jnp.sum(e, axis=-1, keepdims=True)
                o = jnp.dot(e.astype(BF16), v.astype(BF16), preferred_element_type=F32)
                os_ref[g, rows, :] = sel(o)
                ms_ref[g, rows, :] = sel(wide(m))
                ds_ref[g, rows, :] = sel(wide(den))
    m_all = jnp.maximum(jnp.maximum(ms_ref[0], ms_ref[1]), ms_ref[2])
    num = jnp.zeros((SEQ, LANES), F32)
    den = jnp.zeros((SEQ, LANES), F32)
    for g in range(len(DIL_GROUPS)):
        w = jnp.exp(ms_ref[g] - m_all)
        num = num + os_ref[g] * w
        den = den + ds_ref[g] * w
    o_ref[...] = (num / den).astype(o_ref.dtype)


def _dil_attn(q, kv):
    ng = len(DIL_GROUPS)
    pairs = DIL_HEADS // 2
    blk = lambda off: pl.BlockSpec((ng, SEQ, LANES), lambda b, j, off=off: (0, b, off + j))
    return pl.pallas_call(
        _dil_attn_kernel, grid=(BATCH, pairs),
        in_specs=[blk(0), blk(0), blk(pairs)],
        out_specs=pl.BlockSpec((SEQ, LANES), lambda b, j: (b, j)),
        out_shape=jax.ShapeDtypeStruct((N_PROMPT, DIL_WIDTH), BF16),
        scratch_shapes=[pltpu.VMEM((ng, SEQ, LANES), F32)] * 3,
        compiler_params=_cparams(("parallel", "parallel")), name="dil_attn_prompt",
    )(q, kv, kv)


def _dil_sample_kernel(*refs):
    q_ref, kr_ref, vr_ref, kc_ref, vc_ref, s0_ref, s1_ref, s2_ref = refs[:8]
    o_ref, n0_ref, n1_ref, n2_ref = refs[-4:]
    b = pl.program_id(0)
    hb = pl.program_id(1)
    rw = DIL_HEAD_BLOCK * DIL_HEAD_DIM
    rowi = lax.broadcasted_iota(jnp.int32, (8, rw), 0)
    coli = lax.broadcasted_iota(jnp.int32, (8, rw), 1)
    bd = (coli // DIL_HEAD_DIM) == rowi
    head = hb * DIL_HEAD_BLOCK + lax.broadcasted_iota(jnp.int32, (8, 1), 0)
    slope = jnp.exp2(-(8.0 / DIL_HEADS) * (head + 1).astype(F32))
    is_b = lax.broadcasted_iota(jnp.int32, (rw, LANES), 1) == b
    parts = []
    for g, (s_ref, n_ref, (window, d)) in enumerate(zip((s0_ref, s1_ref, s2_ref), (n0_ref, n1_ref, n2_ref), DIL_GROUPS)):
        q_bd = jnp.where(bd, q_ref[g], 0.0)
        k_old = s_ref[0].reshape(rw, window)
        v_old = s_ref[1].reshape(rw, window)
        s = jnp.dot(q_bd.astype(BF16), k_old.astype(BF16), preferred_element_type=F32)
        pos = lax.broadcasted_iota(jnp.int32, (8, window), 1)
        back = (window - pos).astype(F32)
        s = jnp.where((pos & (d - 1)) == 0, s - slope * back, NEG_INF)
        s_new = jnp.sum(q_bd * kr_ref[g], axis=1, keepdims=True)
        m = jnp.maximum(jnp.max(s, axis=1, keepdims=True), s_new)
        e = jnp.exp(s - m)
        e_new = jnp.exp(s_new - m)
        den = jnp.sum(e, axis=1, keepdims=True) + e_new
        o = (lax.dot_general(e.astype(BF16), v_old.astype(BF16), NT, preferred_element_type=F32)
             + e_new * jnp.where(bd, vr_ref[g], 0.0))
        parts.append((o, m, den))
        last = lax.broadcasted_iota(jnp.int32, (rw, window), 1) == window - 1
        for t, (old, c_ref) in enumerate(((k_old, kc_ref), (v_old, vc_ref))):
            col = jnp.sum(jnp.where(is_b, c_ref[g], 0.0), axis=1, keepdims=True)
            new = jnp.where(last, col, pltpu.roll(old, window - 1, 1))
            n_ref[t] = new.reshape(DIL_HEAD_BLOCK, DIL_HEAD_DIM, window)
    m_all = jnp.maximum(jnp.maximum(parts[0][1], parts[1][1]), parts[2][1])
    num = jnp.zeros((8, rw), F32)
    den = jnp.zeros((8, 1), F32)
    for o, m, dn in parts:
        wgt = jnp.exp(m - m_all)
        num = num + o * wgt
        den = den + dn * wgt
    o_ref[...] = jnp.sum(jnp.where(bd, num / den, 0.0), axis=0, keepdims=True).astype(o_ref.dtype)


def _dil_sample(q_s, kv_s, kvt_s, states_t, layer, prev):
    ng = len(DIL_GROUPS)
    w = DIL_WIDTH
    hbk = DIL_HEAD_BLOCK
    rw = hbk * DIL_HEAD_DIM
    nhb = DIL_HEADS // hbk
    row = lambda off: pl.BlockSpec((ng, None, 1, rw), lambda b, hb, off=off: (0, b, 0, off + hb))
    colspec = lambda off: pl.BlockSpec((ng, rw, LANES), lambda b, hb, off=off: (0, off + hb, 0))
    st_spec = lambda window: pl.BlockSpec((None, None, 2, hbk, DIL_HEAD_DIM, window),
                                          lambda b, hb: (layer, b, 0, hb, 0, 0))
    st_specs = [st_spec(window) for window, _ in DIL_GROUPS]
    in_specs = [row(0), row(0), row(nhb), colspec(0), colspec(nhb)] + st_specs
    args = [q_s.reshape(ng, DEC_BATCH, 1, w), kv_s.reshape(ng, DEC_BATCH, 1, 2 * w),
            kv_s.reshape(ng, DEC_BATCH, 1, 2 * w), kvt_s, kvt_s, *states_t]
    aliases = {len(args) + k: 1 + k for k in range(ng)}
    in_specs += [_any_spec()] * ng
    args += list(prev)
    return pl.pallas_call(
        _dil_sample_kernel, grid=(DEC_BATCH, nhb),
        in_specs=in_specs,
        out_specs=[pl.BlockSpec((None, 1, rw), lambda b, hb: (b, 0, hb))] + st_specs,
        out_shape=[jax.ShapeDtypeStruct((DEC_BATCH, 1, w), BF16)]
                  + [jax.ShapeDtypeStruct(st.shape, F32) for st in states_t],
        input_output_aliases=aliases,
        compiler_params=_cparams(("parallel", "parallel")), name="dil_attn_sample",
    )(*args)


def _rope_tables(pos):
    half = MLA_ROPE // 2
    inv = ROPE_THETA ** (-jnp.arange(half, dtype=F32) / half)
    ang = pos.astype(F32)[:, None] * inv[None, :]
    cos, sin = jnp.cos(ang), jnp.sin(ang)
    n = pos.shape[0]
    scale = (MLA_NOPE + MLA_ROPE) ** -0.5 * LOG2E
    zq = jnp.zeros((n, LANES - MLA_NOPE - MLA_ROPE), F32)
    cq = jnp.concatenate([jnp.ones((n, MLA_NOPE), F32), cos, cos, zq], axis=1) * scale
    sq = jnp.concatenate([jnp.zeros((n, MLA_NOPE), F32), -sin, sin, zq], axis=1) * scale
    zk = jnp.zeros((n, LANES - MLA_ROPE), F32)
    ck = jnp.concatenate([cos, cos, zk], axis=1)
    sk = jnp.concatenate([-sin, sin, zk], axis=1)
    return cq, sq, ck, sk


def _mla_weights(w_dq, w_uq, w_dkv, w_uk, w_uv):
    half = MLA_ROPE // 2
    padl = lambda a, wdt: jnp.pad(a, [(0, 0)] * (a.ndim - 1) + [(0, wdt - a.shape[-1])])
    c, r = w_dkv[:, :MLA_KV_LORA], w_dkv[:, MLA_KV_LORA:]
    r_sw = jnp.concatenate([r[:, half:], r[:, :half]], axis=1)
    wa = jnp.concatenate([w_dq, c, padl(r, LANES), padl(r_sw, LANES)], axis=1).astype(BF16)
    uq = w_uq.reshape(MLA_Q_LORA, MLA_HEADS, MLA_NOPE + MLA_ROPE)
    wq1 = padl(uq, LANES).reshape(MLA_Q_LORA, HEAD_PAD).astype(BF16)
    uq_sw = jnp.concatenate([jnp.zeros_like(uq[..., :MLA_NOPE]), uq[..., MLA_NOPE + half:],
                             uq[..., MLA_NOPE:MLA_NOPE + half]], axis=-1)
    wq2 = padl(uq_sw, LANES).reshape(MLA_Q_LORA, HEAD_PAD).astype(BF16)
    wk_top = padl(w_uk, LANES).reshape(MLA_KV_LORA, HEAD_PAD)
    place = jnp.pad(jnp.eye(MLA_ROPE, dtype=F32), ((0, LANES - MLA_ROPE), (MLA_NOPE, LANES - MLA_NOPE - MLA_ROPE)))
    wk = jnp.concatenate([wk_top, jnp.tile(place, (1, MLA_HEADS))], axis=0).astype(BF16)
    wv = w_uv.reshape(MLA_KV_LORA, MLA_HEADS * MLA_VDIM).astype(BF16)
    w_qlat = padl(jnp.transpose(w_uk, (1, 0, 2)), LANES)
    w_qlat = jnp.transpose(w_qlat, (0, 2, 1)).astype(BF16)
    uv = jnp.transpose(w_uv, (1, 0, 2))
    z = jnp.zeros_like(uv)
    even = jnp.concatenate([uv, z], axis=-1)
    odd = jnp.concatenate([z, uv], axis=-1)
    is_even = (jnp.arange(MLA_HEADS) % 2 == 0)[:, None, None]
    w_vpair = jnp.where(is_even, even, odd).astype(BF16)
    return wa, wq1, wq2, wk, wv, w_qlat, w_vpair


def kernel(x_prompt, x_sample, cache_mla_ckv, cache_mla_kpe, page_table, state_dil_kv_w128, state_dil_kv_w512, state_dil_kv_w2048, mla_w_dq, mla_g_q, mla_w_uq, mla_w_dkv, mla_g_kv, mla_w_uk, mla_w_uv, mla_w_o, dil_w_qkv, dil_w_o, norm_mix, norm_ffn, ffn_w_up, ffn_w_down, norm_final):
    tm = 512
    xp = x_prompt.reshape(N_PROMPT, D_MODEL)
    xs = x_sample.reshape(DEC_BATCH, D_MODEL)
    tabs_p = _rope_tables(jnp.arange(SEQ))
    tabs_s = _rope_tables(jnp.full((DEC_BATCH,), PAST_LEN))
    gfin = norm_final.reshape(1, D_MODEL)
    wup = ffn_w_up.astype(BF16)
    wdn = ffn_w_down.astype(BF16)
    kpe_pool_t = jnp.transpose(cache_mla_kpe, (0, 1, 3, 2))
    states_t = [jnp.transpose(st, (0, 1, 3, 4, 5, 2))
                for st in (state_dil_kv_w128, state_dil_kv_w512, state_dil_kv_w2048)]
    ckv_p, kpe_p, ckv_s, kpe_s = [], [], [], []
    for layer in range(DEPTH):
        i = layer // 2
        gmix = norm_mix[layer].reshape(1, D_MODEL)
        if layer % 2 == 0:
            wa, wq1, wq2, wk, wv, w_qlat, w_vpair = _mla_weights(mla_w_dq[i], mla_w_uq[i], mla_w_dkv[i], mla_w_uk[i], mla_w_uv[i])
            gq = mla_g_q[i].reshape(1, MLA_Q_LORA)
            gkv = mla_g_kv[i].reshape(1, MLA_KV_LORA)
            q_p, c_p, k_p, kf_p, v_p = _mla_proj(xp, gmix, wa, gq, wq1, wq2, gkv, tabs_p, wk, wv, tm=tm, with_kv=True)
            q_s, c_s, k_s = _mla_proj(xs, gmix, wa, gq, wq1, wq2, gkv, tabs_s, None, None, tm=DEC_BATCH, with_kv=False)
            qlat_t = jnp.transpose(_mla_qlat(q_s, w_qlat), (1, 2, 0))
            qpe_t = jnp.transpose(q_s.reshape(DEC_BATCH, MLA_HEADS, LANES)[:, :, MLA_NOPE:MLA_NOPE + MLA_ROPE],
                                  (0, 2, 1)).astype(F32)
            o_p, olat, *fills = _mla_fused_attn(q_p, kf_p, v_p, page_table, qlat_t, qpe_t, c_s.T, k_s.T,
                                                cache_mla_ckv, kpe_pool_t, i, tq=512, fill=layer == 0)
            if layer == 0:
                dil_p_bufs, dil_s_bufs = fills[:len(DIL_GROUPS)], fills[len(DIL_GROUPS):]
            o_s = _mla_sample_out(jnp.transpose(olat, (1, 0, 2)), w_vpair)
            wo = mla_w_o[i].astype(BF16)
            ckv_p.append(c_p.reshape(BATCH, SEQ, MLA_KV_LORA))
            kpe_p.append(k_p)
            ckv_s.append(c_s.reshape(DEC_BATCH, 1, MLA_KV_LORA))
            kpe_s.append(k_s.reshape(DEC_BATCH, 1, MLA_ROPE))
        else:
            wqkv = dil_w_qkv[i].astype(BF16)
            q_p, kv_p, *dil_p_bufs = _dil_proj_prompt(xp, gmix, wqkv, i, dil_p_bufs, tm=256)
            q_s, kv_s, kvt_s = _dil_proj_sample(xs, gmix, wqkv)
            o_p = _dil_attn(q_p, kv_p)
            o_s, *dil_s_bufs = _dil_sample(q_s, kv_s, kvt_s, states_t, i, dil_s_bufs)
            o_s = o_s.reshape(DEC_BATCH, DIL_WIDTH)
            wo = dil_w_o[i].astype(BF16)
        final = layer == DEPTH - 1
        g_ffn = norm_ffn[layer].reshape(1, D_MODEL)
        xp = _post_ffn(xp, o_p, wo, g_ffn, wup, wdn, gfin, layer, tm=tm, final=final)
        xs = _post_ffn(xs, o_s, wo, g_ffn, wup, wdn, gfin, layer, tm=DEC_BATCH, final=final)
    nl = DEPTH // 2
    to_window = lambda a, nb: jnp.transpose(
        a.reshape(nl, nb, 2, DIL_HEADS, DIL_HEAD_DIM, a.shape[-1]), (0, 1, 5, 2, 3, 4))
    return (xp.reshape(BATCH, SEQ, D_MODEL), xs.reshape(DEC_BATCH, 1, D_MODEL),
            jnp.stack(ckv_p), jnp.transpose(jnp.stack(kpe_p), (0, 1, 3, 2)), jnp.stack(ckv_s), jnp.stack(kpe_s),
            *[to_window(a, BATCH) for a in dil_p_bufs],
            *[to_window(a, DEC_BATCH) for a in dil_s_bufs])
```

```python
import functools
import math

import jax
import jax.numpy as jnp
from jax import lax
from jax.experimental import pallas as pl
from jax.experimental.pallas import tpu as pltpu

F32 = jnp.float32
BF16 = jnp.bfloat16

D_MODEL = 1024
BATCH = 8
SEQ = 2048
DEPTH = 4
DEC_BATCH = 32
PAST_LEN = 16384
PAGE_SIZE = 128
N_PAGES = PAST_LEN // PAGE_SIZE
MLA_HEADS = 16
MLA_NOPE = 64
MLA_ROPE = 32
MLA_VDIM = 64
MLA_Q_LORA = 384
MLA_KV_LORA = 256
ROPE_THETA = 10000.0
DIL_GROUPS = ((128, 1), (512, 4), (2048, 16))
DIL_SPAN = 128
DIL_HEADS = 8
DIL_HEAD_DIM = 64
DIL_WIDTH = DIL_HEADS * DIL_HEAD_DIM
D_FF = 4 * D_MODEL
EPS = 1e-6
NEG_INF = -1e30
LOG2E = math.log2(math.e)

LANES = 128
N_PROMPT = BATCH * SEQ
HEAD_PAD = MLA_HEADS * LANES
VMEM_LIMIT = 56 * 1024 * 1024
DIL_FUSED_VMEM_LIMIT = 60 * 1024 * 1024
FUSED_PAGES_PER_STEP = 64
PAGES_PER_CHUNK = 2
FILL_COPIES = len(DIL_GROUPS) * (DEPTH // 2) * (2 + DEC_BATCH // BATCH)
DIL_HEAD_BLOCK = 4
NT = (((1,), (1,)), ((), ()))
TN = (((0,), (0,)), ((), ()))


def _cparams(sem):
    return pltpu.CompilerParams(dimension_semantics=sem, vmem_limit_bytes=VMEM_LIMIT)


def _rms(x, g):
    return x * lax.rsqrt(jnp.mean(x * x, axis=-1, keepdims=True) + EPS) * g


def _const_spec(shape):
    nd = len(shape)
    return pl.BlockSpec(shape, lambda *_: (0,) * nd, pipeline_mode=pl.Buffered(1))


def _any_spec():
    return pl.BlockSpec(memory_space=pl.ANY)


def _mla_proj_kernel(*refs, with_kv):
    if with_kv:
        (x_ref, gmix_ref, wa_ref, gq_ref, wq1_ref, wq2_ref, gkv_ref, cq_ref, sq_ref, ck_ref, sk_ref,
         wk_ref, wv_ref, q_out, ckv_out, kpe_out, k_out, v_out) = refs
    else:
        (x_ref, gmix_ref, wa_ref, gq_ref, wq1_ref, wq2_ref, gkv_ref, cq_ref, sq_ref, ck_ref, sk_ref,
         q_out, ckv_out, kpe_out) = refs
    h = _rms(x_ref[...], gmix_ref[...]).astype(BF16)
    a = jnp.dot(h, wa_ref[...], preferred_element_type=F32)
    c_q = _rms(a[:, :MLA_Q_LORA], gq_ref[...]).astype(BF16)
    o1 = MLA_Q_LORA + MLA_KV_LORA
    c_kv = _rms(a[:, MLA_Q_LORA:o1], gkv_ref[...])
    k_pe = a[:, o1:o1 + LANES] * ck_ref[...] + a[:, o1 + LANES:o1 + 2 * LANES] * sk_ref[...]
    ckv_out[...] = c_kv
    if with_kv:
        kpe_out[...] = k_pe.T[:MLA_ROPE, :]
    else:
        kpe_out[...] = k_pe[:, :MLA_ROPE]
    heads_per_chunk = 4
    cw = heads_per_chunk * LANES
    cq_c = jnp.concatenate([cq_ref[...]] * heads_per_chunk, axis=1)
    sq_c = jnp.concatenate([sq_ref[...]] * heads_per_chunk, axis=1)
    for c in range(MLA_HEADS // heads_per_chunk):
        q1 = jnp.dot(c_q, wq1_ref[:, c * cw:(c + 1) * cw], preferred_element_type=F32)
        q2 = jnp.dot(c_q, wq2_ref[:, c * cw:(c + 1) * cw], preferred_element_type=F32)
        q_out[:, c * cw:(c + 1) * cw] = (q1 * cq_c + q2 * sq_c).astype(q_out.dtype)
    if with_kv:
        c_kv_b = c_kv.astype(BF16)
        lhs = jnp.concatenate([c_kv_b, k_pe.astype(BF16)], axis=1)
        k_out[...] = jnp.dot(lhs, wk_ref[...], preferred_element_type=F32).astype(k_out.dtype)
        v_out[...] = jnp.dot(c_kv_b, wv_ref[...], preferred_element_type=F32).astype(v_out.dtype)


def _mla_proj(x, gmix, wa, gq, wq1, wq2, gkv, tabs, wk, wv, *, tm, with_kv):
    n = x.shape[0]
    nt = n // tm
    tab_blocks = tabs[0].shape[0] // tm
    row = lambda w: pl.BlockSpec((tm, w), lambda i: (i, 0))
    tab = pl.BlockSpec((tm, LANES), lambda i: (i % tab_blocks, 0))
    in_specs = [row(D_MODEL), _const_spec(gmix.shape), _const_spec(wa.shape), _const_spec(gq.shape),
                _const_spec(wq1.shape), _const_spec(wq2.shape), _const_spec(gkv.shape), tab, tab, tab, tab]
    args = [x, gmix, wa, gq, wq1, wq2, gkv, *tabs]
    out_shape = [jax.ShapeDtypeStruct((n, HEAD_PAD), BF16), jax.ShapeDtypeStruct((n, MLA_KV_LORA), F32)]
    out_specs = [row(HEAD_PAD), row(MLA_KV_LORA)]
    if with_kv:
        per_b = SEQ // tm
        in_specs += [_const_spec(wk.shape), _const_spec(wv.shape)]
        args += [wk, wv]
        out_shape += [jax.ShapeDtypeStruct((BATCH, MLA_ROPE, SEQ), F32),
                      jax.ShapeDtypeStruct((n, HEAD_PAD), BF16),
                      jax.ShapeDtypeStruct((n, MLA_HEADS * MLA_VDIM), BF16)]
        out_specs += [pl.BlockSpec((None, MLA_ROPE, tm), lambda i: (i // per_b, 0, i % per_b)),
                      row(HEAD_PAD), row(MLA_HEADS * MLA_VDIM)]
    else:
        out_shape.append(jax.ShapeDtypeStruct((n, MLA_ROPE), F32))
        out_specs.append(row(MLA_ROPE))
    return pl.pallas_call(
        functools.partial(_mla_proj_kernel, with_kv=with_kv),
        grid=(nt,), in_specs=in_specs, out_specs=out_specs, out_shape=out_shape,
        compiler_params=_cparams(("parallel",)), name="mla_proj_kv" if with_kv else "mla_proj_q",
    )(*args)


def _prompt_attn_body(q_ref, k_ref, v_ref, o_ref, tq):
    row = lax.broadcasted_iota(jnp.int32, (tq, tq), 0)
    col = lax.broadcasted_iota(jnp.int32, (tq, tq), 1)
    causal = col <= row
    lane = lax.broadcasted_iota(jnp.int32, (tq, LANES), 1)
    for i in reversed(range(SEQ // tq)):
        w0 = i * tq
        outs = []
        for hh in range(2):
            hs = slice(hh * LANES, (hh + 1) * LANES)
            q = q_ref[w0:w0 + tq, hs]
            s_d = lax.dot_general(q, k_ref[w0:w0 + tq, hs], NT, preferred_element_type=F32)
            s_d = jnp.where(causal, s_d, NEG_INF)
            m = jnp.max(s_d, axis=-1, keepdims=True)
            if i > 0:
                s_o = lax.dot_general(q, k_ref[0:w0, hs], NT, preferred_element_type=F32)
                m = jnp.maximum(m, jnp.max(s_o, axis=-1, keepdims=True))
            p_d = jnp.exp2(s_d - m)
            l = jnp.sum(p_d, axis=-1, keepdims=True)
            acc = jnp.dot(p_d.astype(BF16), v_ref[w0:w0 + tq, :], preferred_element_type=F32)
            if i > 0:
                p_o = jnp.exp2(s_o - m)
                l = l + jnp.sum(p_o, axis=-1, keepdims=True)
                acc = acc + jnp.dot(p_o.astype(BF16), v_ref[0:w0, :], preferred_element_type=F32)
            outs.append(acc / l)
        o_ref[w0:w0 + tq, :] = jnp.where(lane < MLA_VDIM, outs[0], outs[1]).astype(o_ref.dtype)


def _window_fill_shapes():
    nl = DEPTH // 2
    prompt = [jax.ShapeDtypeStruct((nl, BATCH, 2 * DIL_WIDTH, window), F32) for window, _ in DIL_GROUPS]
    sample = [jax.ShapeDtypeStruct((nl, DEC_BATCH, 2, DIL_HEADS, DIL_HEAD_DIM, window), F32)
              for window, _ in DIL_GROUPS]
    return prompt + sample


def _mla_qlat_kernel(q_ref, w_ref, o_ref):
    for h in range(MLA_HEADS):
        o_ref[h] = jnp.dot(q_ref[:, h * LANES:(h + 1) * LANES], w_ref[h], preferred_element_type=F32)


def _mla_qlat(q_s, w_qlat):
    return pl.pallas_call(
        _mla_qlat_kernel,
        out_shape=jax.ShapeDtypeStruct((MLA_HEADS, DEC_BATCH, MLA_KV_LORA), F32),
        compiler_params=pltpu.CompilerParams(vmem_limit_bytes=VMEM_LIMIT), name="mla_qlat",
    )(q_s, w_qlat)


def _sample_attn_init(batch, qlat_ref, qpe_ref, cnew_ref, knew_ref, cnew_row_ref, m_s, l_s, acc_s):
    pick = lambda ref: jnp.sum(
        jnp.where(lax.broadcasted_iota(jnp.int32, ref.shape, 1) == batch, ref[...], 0.0), axis=1, keepdims=True)
    s_new = (jnp.sum(qlat_ref[...] * pick(cnew_ref), axis=0, keepdims=True)
             + jnp.sum(qpe_ref[...] * pick(knew_ref), axis=0, keepdims=True))
    m_s[...] = s_new
    l_s[...] = jnp.ones_like(s_new)
    acc_s[...] = jnp.broadcast_to(cnew_row_ref[...], acc_s.shape)


def _heads_to_col(r):
    eye = (lax.broadcasted_iota(jnp.int32, (MLA_HEADS, MLA_HEADS), 0)
           == lax.broadcasted_iota(jnp.int32, (MLA_HEADS, MLA_HEADS), 1))
    return jnp.sum(jnp.where(eye, r, 0.0), axis=1, keepdims=True)


def _sample_attn_pages(qlat_ref, qpe_ref, ckv_refs, kpe_refs, m_s, l_s, acc_s, per):
    qlat_b = qlat_ref[...].astype(BF16)
    qpe_b = qpe_ref[...].astype(BF16)
    m_old = m_s[...]
    m_new = m_old
    parts = []
    for j in range(len(ckv_refs) // per):
        ckv = jnp.concatenate([r[...] for r in ckv_refs[j * per:(j + 1) * per]], axis=0).astype(BF16)
        kpe_t = jnp.concatenate([r[...] for r in kpe_refs[j * per:(j + 1) * per]], axis=1).astype(BF16)
        s = (jnp.dot(ckv, qlat_b, preferred_element_type=F32)
             + lax.dot_general(kpe_t, qpe_b, TN, preferred_element_type=F32))
        m_new = jnp.maximum(m_new, jnp.max(s, axis=0, keepdims=True))
        parts.append((s, ckv))
    alpha = jnp.exp2(m_old - m_new)
    l = alpha * l_s[...]
    acc = _heads_to_col(alpha) * acc_s[...]
    for s, ckv in parts:
        p = jnp.exp2(s - m_new)
        l = l + jnp.sum(p, axis=0, keepdims=True)
        acc = acc + lax.dot_general(p.astype(BF16), ckv, TN, preferred_element_type=F32)
    m_s[...] = m_new
    l_s[...] = l
    acc_s[...] = acc
    return l, acc


def _mla_fused_attn_kernel(pt_ref, q_ref, k_ref, v_ref, qlat_ref, qpe_ref, cnew_ref, knew_ref, cnew_row_ref,
                           *rest, tq, npg, fill):
    del pt_ref
    ckv_refs = rest[:npg]
    kpe_refs = rest[npg:2 * npg]
    rest = rest[2 * npg:]
    b = pl.program_id(0)
    j = pl.program_id(1)
    t = b * pl.num_programs(1) + j
    steps_per_row = N_PAGES // npg
    copies = []
    if fill:
        o_ref, olat_ref, *bufs, m_s, l_s, acc_s, zero_s, sems = rest

        @pl.when(t == 0)
        def _():
            zero_s[...] = jnp.zeros(zero_s.shape, zero_s.dtype)

        ng = len(DIL_GROUPS)
        per_b = DEC_BATCH // BATCH
        for g, (window, _) in enumerate(DIL_GROUPS):
            cols = pl.ds(0, window)
            for layer in range(DEPTH // 2):
                for half in range(2):
                    rows = pl.ds(j * (2 * DIL_HEAD_DIM) + half * DIL_HEAD_DIM, DIL_HEAD_DIM)
                    copies.append((zero_s.at[0, :, cols], bufs[g].at[layer, b, rows, :]))
                for bb in range(per_b):
                    copies.append((zero_s.at[:, :, cols], bufs[ng + g].at[layer, b * per_b + bb, :, j]))
        copies = [pltpu.make_async_copy(src, dst, sems.at[n]) for n, (src, dst) in enumerate(copies)]
        assert len(copies) == FILL_COPIES
        for c in copies:
            c.start()
    else:
        o_ref, olat_ref, m_s, l_s, acc_s = rest

    @pl.when(t % steps_per_row == 0)
    def _():
        _sample_attn_init(t // steps_per_row, qlat_ref, qpe_ref, cnew_ref, knew_ref, cnew_row_ref, m_s, l_s, acc_s)

    l, acc = _sample_attn_pages(qlat_ref, qpe_ref, ckv_refs, kpe_refs, m_s, l_s, acc_s, PAGES_PER_CHUNK)
    _prompt_attn_body(q_ref, k_ref, v_ref, o_ref, tq)
    for c in copies:
        c.wait()

    @pl.when(t % steps_per_row == steps_per_row - 1)
    def _():
        olat_ref[...] = acc / _heads_to_col(l)


def _mla_fused_attn(q, k, v, page_table, qlat_t, qpe_t, cnew_t, knew_t, ckv_pool, kpe_pool_t, layer, *, tq, fill):
    pairs = MLA_HEADS // 2
    npg = FUSED_PAGES_PER_STEP
    steps_per_row = N_PAGES // npg
    assert BATCH * pairs * npg == DEC_BATCH * N_PAGES and pairs == DIL_HEADS
    fill_shapes = _window_fill_shapes() if fill else []
    fill_scratch = [pltpu.VMEM((2, DIL_HEAD_DIM, max(w for w, _ in DIL_GROUPS)), F32),
                    pltpu.SemaphoreType.DMA((FILL_COPIES,))] if fill else []
    srow = lambda b, j: (b * pairs + j) // steps_per_row
    prompt = lambda w: pl.BlockSpec((SEQ, w), lambda b, j, pt: (b, j))
    per_row = lambda r, w: pl.BlockSpec((None, r, w), lambda b, j, pt: (srow(b, j), 0, 0))
    whole = lambda r: pl.BlockSpec((r, DEC_BATCH), lambda b, j, pt: (0, 0))
    page = lambda r, w, kk: pl.BlockSpec(
        (None, None, r, w),
        lambda b, j, pt, kk=kk: (layer, pt[srow(b, j), ((b * pairs + j) % steps_per_row) * npg + kk], 0, 0))
    in_specs = ([prompt(2 * LANES), prompt(2 * LANES), prompt(LANES),
                 per_row(MLA_KV_LORA, MLA_HEADS), per_row(MLA_ROPE, MLA_HEADS), whole(MLA_KV_LORA), whole(MLA_ROPE),
                 per_row(1, MLA_KV_LORA)]
                + [page(PAGE_SIZE, MLA_KV_LORA, kk) for kk in range(npg)]
                + [page(MLA_ROPE, PAGE_SIZE, kk) for kk in range(npg)])
    grid_spec = pltpu.PrefetchScalarGridSpec(
        num_scalar_prefetch=1, grid=(BATCH, pairs), in_specs=in_specs,
        out_specs=[pl.BlockSpec((SEQ, LANES), lambda b, j, pt: (b, j)),
                   pl.BlockSpec((None, MLA_HEADS, MLA_KV_LORA), lambda b, j, pt: (srow(b, j), 0, 0))]
                  + [_any_spec()] * len(fill_shapes),
        scratch_shapes=[pltpu.VMEM((1, MLA_HEADS), F32), pltpu.VMEM((1, MLA_HEADS), F32),
                        pltpu.VMEM((MLA_HEADS, MLA_KV_LORA), F32)] + fill_scratch)
    return pl.pallas_call(
        functools.partial(_mla_fused_attn_kernel, tq=tq, npg=npg, fill=fill), grid_spec=grid_spec,
        out_shape=[jax.ShapeDtypeStruct((N_PROMPT, MLA_HEADS * MLA_VDIM), BF16),
                   jax.ShapeDtypeStruct((DEC_BATCH, MLA_HEADS, MLA_KV_LORA), F32)] + fill_shapes,
        compiler_params=_cparams(("arbitrary", "arbitrary")), name="mla_attn_fused",
    )(page_table, q, k, v, qlat_t, qpe_t, cnew_t, knew_t, cnew_t.T.reshape(DEC_BATCH, 1, MLA_KV_LORA),
      *([ckv_pool] * npg), *([kpe_pool_t] * npg))


def _mla_sample_out_kernel(olat_ref, w_ref, o_ref):
    for j in range(MLA_HEADS // 2):
        acc = None
        for h in (2 * j, 2 * j + 1):
            t = jnp.dot(olat_ref[h].astype(BF16), w_ref[h], preferred_element_type=F32)
            acc = t if acc is None else acc + t
        o_ref[:, j * LANES:(j + 1) * LANES] = acc.astype(o_ref.dtype)


def _mla_sample_out(olat_t, w_vpair):
    return pl.pallas_call(
        _mla_sample_out_kernel,
        out_shape=jax.ShapeDtypeStruct((DEC_BATCH, MLA_HEADS * MLA_VDIM), BF16),
        compiler_params=pltpu.CompilerParams(vmem_limit_bytes=VMEM_LIMIT), name="mla_sample_out",
    )(olat_t, w_vpair)


def _post_ffn_kernel(*refs, final, ff_chunk):
    if final:
        x_ref, o_ref, wo_ref, g_ref, wup_ref, wdn_ref, gfin_ref, out_ref = refs
    else:
        x_ref, o_ref, wo_ref, g_ref, wup_ref, wdn_ref, out_ref = refs
    x1 = x_ref[...] + jnp.dot(o_ref[...], wo_ref[...], preferred_element_type=F32)
    hn = _rms(x1, g_ref[...]).astype(BF16)
    acc = x1
    for c in range(D_FF // ff_chunk):
        u = jnp.dot(hn, wup_ref[:, c * ff_chunk:(c + 1) * ff_chunk], preferred_element_type=F32)
        u = jnp.square(jnp.maximum(u, 0.0)).astype(BF16)
        acc = acc + jnp.dot(u, wdn_ref[c * ff_chunk:(c + 1) * ff_chunk, :], preferred_element_type=F32)
    if final:
        acc = _rms(acc, gfin_ref[...])
    out_ref[...] = acc


def _post_ffn(x, o, wo, g, wup, wdn, gfin, layer, *, tm, final):
    n = x.shape[0]
    ko = o.shape[1]
    row = lambda w: pl.BlockSpec((tm, w), lambda i: (i, 0))
    per_layer = lambda a: pl.BlockSpec((None,) + a.shape[1:], lambda i: (layer, 0, 0), pipeline_mode=pl.Buffered(1))
    in_specs = [row(D_MODEL), row(ko), _const_spec(wo.shape), _const_spec(g.shape),
                per_layer(wup), per_layer(wdn)]
    args = [x, o, wo, g, wup, wdn]
    if final:
        in_specs.append(_const_spec(gfin.shape))
        args.append(gfin)
    return pl.pallas_call(
        functools.partial(_post_ffn_kernel, final=final, ff_chunk=1024),
        grid=(n // tm,), in_specs=in_specs, out_specs=row(D_MODEL),
        out_shape=jax.ShapeDtypeStruct((n, D_MODEL), F32),
        compiler_params=_cparams(("parallel",)), name="post_ffn_final" if final else "post_ffn",
    )(*args)


def _dil_proj_prompt_kernel(*refs):
    x_ref, g_ref, w_ref = refs[:3]
    q_out, kv_out, b128_out, b512_out, b2048_out = refs[-5:]
    h = _rms(x_ref[...], g_ref[...]).astype(BF16)
    gw = 3 * DIL_WIDTH
    tm = x_ref.shape[0]
    for g, buf_out in enumerate((b128_out, b512_out, b2048_out)):
        r = jnp.dot(h, w_ref[:, g * gw:(g + 1) * gw], preferred_element_type=F32)
        q_out[g] = r[:, :DIL_WIDTH] * (DIL_HEAD_DIM ** -0.5 * LOG2E)
        kv = r[:, DIL_WIDTH:]
        kv_out[g] = kv
        kv_t = kv.T
        wb = buf_out.shape[-1]
        buf_out[...] = kv_t[:, tm - wb:] if wb < tm else kv_t


def _dil_proj_prompt(x, g, w, layer, prev, *, tm):
    n = x.shape[0]
    ng = len(DIL_GROUPS)
    per_b = SEQ // tm
    kvw = 2 * DIL_WIDTH
    buf_shapes, buf_specs = [], []
    for window, _ in DIL_GROUPS:
        wb = min(window, tm)
        nblk = window // wb
        first = per_b - nblk
        buf_shapes.append(jax.ShapeDtypeStruct((DEPTH // 2, BATCH, kvw, window), F32))
        buf_specs.append(pl.BlockSpec(
            (None, None, kvw, wb),
            lambda i, first=first: (layer, i // per_b, 0, jnp.maximum(i % per_b - first, 0))))
    in_specs = [pl.BlockSpec((tm, D_MODEL), lambda i: (i, 0)), _const_spec(g.shape), _const_spec(w.shape)]
    in_specs += [_any_spec()] * ng
    args = [x, g, w, *prev]
    aliases = {3 + k: 2 + k for k in range(ng)}
    return pl.pallas_call(
        _dil_proj_prompt_kernel, grid=(n // tm,),
        in_specs=in_specs,
        out_specs=[pl.BlockSpec((ng, tm, DIL_WIDTH), lambda i: (0, i, 0)),
                   pl.BlockSpec((ng, tm, kvw), lambda i: (0, i, 0))] + buf_specs,
        out_shape=[jax.ShapeDtypeStruct((ng, n, DIL_WIDTH), F32), jax.ShapeDtypeStruct((ng, n, kvw), F32)] + buf_shapes,
        input_output_aliases=aliases,
        compiler_params=_cparams(("arbitrary",)), name="dil_proj_prompt",
    )(*args)


def _dil_proj_sample_kernel(x_ref, g_ref, w_ref, q_out, kv_out, kvt_out):
    h = _rms(x_ref[...], g_ref[...]).astype(BF16)
    gw = 3 * DIL_WIDTH
    pad = jnp.zeros((LANES - DEC_BATCH, 2 * DIL_WIDTH), F32)
    for g in range(len(DIL_GROUPS)):
        r = jnp.dot(h, w_ref[:, g * gw:(g + 1) * gw], preferred_element_type=F32)
        q_out[g] = r[:, :DIL_WIDTH] * (DIL_HEAD_DIM ** -0.5)
        kv = r[:, DIL_WIDTH:]
        kv_out[g] = kv
        kvt_out[g] = jnp.concatenate([kv, pad], axis=0).T


def _dil_proj_sample(x, g, w):
    ng = len(DIL_GROUPS)
    return pl.pallas_call(
        _dil_proj_sample_kernel,
        out_shape=[jax.ShapeDtypeStruct((ng, DEC_BATCH, DIL_WIDTH), F32),
                   jax.ShapeDtypeStruct((ng, DEC_BATCH, 2 * DIL_WIDTH), F32),
                   jax.ShapeDtypeStruct((ng, 2 * DIL_WIDTH, LANES), F32)],
        compiler_params=pltpu.CompilerParams(vmem_limit_bytes=VMEM_LIMIT), name="dil_proj_sample",
    )(x, g, w)


def _strided(start, size, stride):
    return pl.ds(start, size) if stride == 1 else pl.ds(start, size, stride=stride)


def _dil_prompt_combos(q_ref, k_ref, v_ref, os_ref, ms_ref, ds_ref, j, work):
    sp = DIL_SPAN
    lane = lax.broadcasted_iota(jnp.int32, (sp, LANES), 1)
    first_head = lane < DIL_HEAD_DIM
    jq = lax.broadcasted_iota(jnp.int32, (sp, 2 * sp), 0)
    jk = lax.broadcasted_iota(jnp.int32, (sp, 2 * sp), 1)
    dist = jq + sp - jk
    valid = (dist >= 0) & (dist <= sp)
    slopes = [LOG2E * jnp.exp2(-(8.0 / DIL_HEADS) * (2 * j + hh + 1).astype(F32)) for hh in range(2)]
    sel = lambda a: jnp.where(first_head, a[:sp], a[sp:])
    wide = lambda a: jnp.broadcast_to(a, (2 * sp, LANES))
    own_lanes = ((lax.broadcasted_iota(jnp.int32, (2 * sp, LANES), 1) < DIL_HEAD_DIM)
                 == (lax.broadcasted_iota(jnp.int32, (2 * sp, LANES), 0) < sp))
    for g, residues in work:
        d = DIL_GROUPS[g][1]
        dist_f = (dist * d).astype(F32)
        bias_full = jnp.concatenate([jnp.where(valid, -slopes[hh] * dist_f, NEG_INF) for hh in range(2)], axis=0)
        bias_cur = bias_full[:, sp:]
        for r in residues:
            k_prev = v_prev = None
            for n in range(SEQ // (sp * d)):
                qs = n * sp * d + r
                rows = _strided(qs, sp, d)
                q = q_ref[g, rows, :]
                k_cur = k_ref[g, rows, :].astype(BF16)
                v_cur = v_ref[g, rows, :].astype(BF16)
                k = k_cur if n == 0 else jnp.concatenate([k_prev, k_cur], axis=0)
                v = v_cur if n == 0 else jnp.concatenate([v_prev, v_cur], axis=0)
                k_prev, v_prev = k_cur, v_cur
                q2 = jnp.where(own_lanes, jnp.concatenate([q, q], axis=0), 0.0).astype(BF16)
                s = lax.dot_general(q2, k, NT, preferred_element_type=F32)
                s = s + (bias_full if n > 0 else bias_cur)
                m = jnp.max(s, axis=-1, keepdims=True)
                e = jnp.exp2(s - m)
                den = jnp.sum(e, axis=-1, keepdims=True)
                o = jnp.dot(e.astype(BF16), v, preferred_element_type=F32)
                os_ref[g, rows, :] = sel(o)
                ms_ref[g, rows, :] = sel(wide(m))
                ds_ref[g, rows, :] = sel(wide(den))
def _dil_prompt_merge(o_ref, os_ref, ms_ref, ds_ref):
    m_all = jnp.maximum(jnp.maximum(ms_ref[0], ms_ref[1]), ms_ref[2])
    num = jnp.zeros((SEQ, LANES), F32)
    den = jnp.zeros((SEQ, LANES), F32)
    for g in range(len(DIL_GROUPS)):
        w = jnp.exp2(ms_ref[g] - m_all)
        num = num + os_ref[g] * w
        den = den + ds_ref[g] * w
    o_ref[...] = (num / den).astype(o_ref.dtype)


def _dil_sample_body(b, hb, q_ref, kr_ref, vr_ref, kc_ref, vc_ref, s_refs, o_ref, n_refs):
    s0_ref, s1_ref, s2_ref = s_refs
    n0_ref, n1_ref, n2_ref = n_refs
    rw = DIL_HEAD_BLOCK * DIL_HEAD_DIM
    rowi = lax.broadcasted_iota(jnp.int32, (8, rw), 0)
    coli = lax.broadcasted_iota(jnp.int32, (8, rw), 1)
    bd = (coli // DIL_HEAD_DIM) == rowi
    head = hb * DIL_HEAD_BLOCK + lax.broadcasted_iota(jnp.int32, (8, 1), 0)
    slope = jnp.exp2(-(8.0 / DIL_HEADS) * (head + 1).astype(F32))
    is_b = lax.broadcasted_iota(jnp.int32, (rw, LANES), 1) == b
    parts = []
    for g, (s_ref, n_ref, (window, d)) in enumerate(zip((s0_ref, s1_ref, s2_ref), (n0_ref, n1_ref, n2_ref), DIL_GROUPS)):
        q_bd = jnp.where(bd, q_ref[g], 0.0)
        k_old = s_ref[0].reshape(rw, window)
        v_old = s_ref[1].reshape(rw, window)
        s = jnp.dot(q_bd.astype(BF16), k_old.astype(BF16), preferred_element_type=F32)
        pos = lax.broadcasted_iota(jnp.int32, (8, window), 1)
        back = (window - pos).astype(F32)
        s = jnp.where((pos & (d - 1)) == 0, s - slope * back, NEG_INF)
        s_new = jnp.sum(q_bd * kr_ref[g], axis=1, keepdims=True)
        m = jnp.maximum(jnp.max(s, axis=1, keepdims=True), s_new)
        e = jnp.exp(s - m)
        e_new = jnp.exp(s_new - m)
        den = jnp.sum(e, axis=1, keepdims=True) + e_new
        o = (lax.dot_general(e.astype(BF16), v_old.astype(BF16), NT, preferred_element_type=F32)
             + e_new * jnp.where(bd, vr_ref[g], 0.0))
        parts.append((o, m, den))
        last = lax.broadcasted_iota(jnp.int32, (rw, window), 1) == window - 1
        for t, (old, c_ref) in enumerate(((k_old, kc_ref), (v_old, vc_ref))):
            col = jnp.sum(jnp.where(is_b, c_ref[g], 0.0), axis=1, keepdims=True)
            new = jnp.where(last, col, pltpu.roll(old, window - 1, 1))
            n_ref[t] = new.reshape(DIL_HEAD_BLOCK, DIL_HEAD_DIM, window)
    m_all = jnp.maximum(jnp.maximum(parts[0][1], parts[1][1]), parts[2][1])
    num = jnp.zeros((8, rw), F32)
    den = jnp.zeros((8, 1), F32)
    for o, m, dn in parts:
        wgt = jnp.exp(m - m_all)
        num = num + o * wgt
        den = den + dn * wgt
    o_ref[...] = jnp.sum(jnp.where(bd, num / den, 0.0), axis=0, keepdims=True).astype(o_ref.dtype)


def _dil_fused_kernel(*refs):
    q_ref, k_ref, v_ref, qs_ref, kr_ref, vr_ref, kc_ref, vc_ref = refs[:8]
    s_refs = refs[8:11]
    o_ref, os_out = refs[14:16]
    n_refs = refs[16:19]
    os_ref, ms_ref, ds_ref = refs[19:]
    b, j, h = pl.program_id(0), pl.program_id(1), pl.program_id(2)
    row = b * pl.num_programs(1) + j
    ng = len(DIL_GROUPS)
    last_d = DIL_GROUPS[ng - 1][1]
    first = [(0, range(DIL_GROUPS[0][1])), (ng - 1, range(last_d // 2))]
    second = [(g, range(DIL_GROUPS[g][1])) for g in range(1, ng - 1)] + [(ng - 1, range(last_d // 2, last_d))]

    @pl.when(h == 0)
    def _():
        _dil_prompt_combos(q_ref, k_ref, v_ref, os_ref, ms_ref, ds_ref, j, first)
        _dil_sample_body(row, h, qs_ref, kr_ref, vr_ref, kc_ref, vc_ref, s_refs, os_out, n_refs)

    @pl.when(h == 1)
    def _():
        _dil_prompt_combos(q_ref, k_ref, v_ref, os_ref, ms_ref, ds_ref, j, second)
        _dil_prompt_merge(o_ref, os_ref, ms_ref, ds_ref)
        _dil_sample_body(row, h, qs_ref, kr_ref, vr_ref, kc_ref, vc_ref, s_refs, os_out, n_refs)


def _dil_fused(q, kv, q_s, kv_s, kvt_s, states_t, layer, prev):
    ng = len(DIL_GROUPS)
    w = DIL_WIDTH
    pairs = DIL_HEADS // 2
    hbk = DIL_HEAD_BLOCK
    rw = hbk * DIL_HEAD_DIM
    nhb = DIL_HEADS // hbk
    assert BATCH * pairs == DEC_BATCH and nhb == 2
    srow = lambda b, j: b * pairs + j
    blk = lambda off: pl.BlockSpec((ng, SEQ, LANES), lambda b, j, h, off=off: (0, b, off + j))
    row = lambda off: pl.BlockSpec((ng, None, 1, rw), lambda b, j, h, off=off: (0, srow(b, j), 0, off + h))
    colspec = lambda off: pl.BlockSpec((ng, rw, LANES), lambda b, j, h, off=off: (0, off + h, 0))
    st_spec = lambda window: pl.BlockSpec((None, None, 2, hbk, DIL_HEAD_DIM, window),
                                          lambda b, j, h: (layer, srow(b, j), 0, h, 0, 0))
    st_specs = [st_spec(window) for window, _ in DIL_GROUPS]
    in_specs = ([blk(0), blk(0), blk(pairs), row(0), row(0), row(nhb), colspec(0), colspec(nhb)]
                + st_specs + [_any_spec()] * ng)
    args = [q, kv, kv, q_s.reshape(ng, DEC_BATCH, 1, w), kv_s.reshape(ng, DEC_BATCH, 1, 2 * w),
            kv_s.reshape(ng, DEC_BATCH, 1, 2 * w), kvt_s, kvt_s, *states_t, *prev]
    aliases = {len(args) - ng + k: 2 + k for k in range(ng)}
    return pl.pallas_call(
        _dil_fused_kernel, grid=(BATCH, pairs, nhb),
        in_specs=in_specs,
        out_specs=[pl.BlockSpec((SEQ, LANES), lambda b, j, h: (b, j)),
                   pl.BlockSpec((None, 1, rw), lambda b, j, h: (srow(b, j), 0, h))] + st_specs,
        out_shape=[jax.ShapeDtypeStruct((N_PROMPT, w), BF16), jax.ShapeDtypeStruct((DEC_BATCH, 1, w), BF16)]
                  + [jax.ShapeDtypeStruct(st.shape, F32) for st in states_t],
        scratch_shapes=[pltpu.VMEM((ng, SEQ, LANES), F32)] * 3,
        input_output_aliases=aliases,
        compiler_params=pltpu.CompilerParams(dimension_semantics=("arbitrary",) * 3,
                                             vmem_limit_bytes=DIL_FUSED_VMEM_LIMIT),
        name="dil_attn_fused",
    )(*args)


def _rope_tables(pos):
    half = MLA_ROPE // 2
    inv = ROPE_THETA ** (-jnp.arange(half, dtype=F32) / half)
    ang = pos.astype(F32)[:, None] * inv[None, :]
    cos, sin = jnp.cos(ang), jnp.sin(ang)
    n = pos.shape[0]
    scale = (MLA_NOPE + MLA_ROPE) ** -0.5 * LOG2E
    zq = jnp.zeros((n, LANES - MLA_NOPE - MLA_ROPE), F32)
    cq = jnp.concatenate([jnp.ones((n, MLA_NOPE), F32), cos, cos, zq], axis=1) * scale
    sq = jnp.concatenate([jnp.zeros((n, MLA_NOPE), F32), -sin, sin, zq], axis=1) * scale
    zk = jnp.zeros((n, LANES - MLA_ROPE), F32)
    ck = jnp.concatenate([cos, cos, zk], axis=1)
    sk = jnp.concatenate([-sin, sin, zk], axis=1)
    return cq, sq, ck, sk


def _mla_weights(w_dq, w_uq, w_dkv, w_uk, w_uv):
    half = MLA_ROPE // 2
    padl = lambda a, wdt: jnp.pad(a, [(0, 0)] * (a.ndim - 1) + [(0, wdt - a.shape[-1])])
    c, r = w_dkv[:, :MLA_KV_LORA], w_dkv[:, MLA_KV_LORA:]
    r_sw = jnp.concatenate([r[:, half:], r[:, :half]], axis=1)
    wa = jnp.concatenate([w_dq, c, padl(r, LANES), padl(r_sw, LANES)], axis=1).astype(BF16)
    uq = w_uq.reshape(MLA_Q_LORA, MLA_HEADS, MLA_NOPE + MLA_ROPE)
    wq1 = padl(uq, LANES).reshape(MLA_Q_LORA, HEAD_PAD).astype(BF16)
    uq_sw = jnp.concatenate([jnp.zeros_like(uq[..., :MLA_NOPE]), uq[..., MLA_NOPE + half:],
                             uq[..., MLA_NOPE:MLA_NOPE + half]], axis=-1)
    wq2 = padl(uq_sw, LANES).reshape(MLA_Q_LORA, HEAD_PAD).astype(BF16)
    wk_top = padl(w_uk, LANES).reshape(MLA_KV_LORA, HEAD_PAD)
    place = jnp.pad(jnp.eye(MLA_ROPE, dtype=F32), ((0, LANES - MLA_ROPE), (MLA_NOPE, LANES - MLA_NOPE - MLA_ROPE)))
    wk = jnp.concatenate([wk_top, jnp.tile(place, (1, MLA_HEADS))], axis=0).astype(BF16)
    wv = w_uv.reshape(MLA_KV_LORA, MLA_HEADS * MLA_VDIM).astype(BF16)
    w_qlat = padl(jnp.transpose(w_uk, (1, 0, 2)), LANES)
    w_qlat = jnp.transpose(w_qlat, (0, 2, 1)).astype(BF16)
    uv = jnp.transpose(w_uv, (1, 0, 2))
    z = jnp.zeros_like(uv)
    even = jnp.concatenate([uv, z], axis=-1)
    odd = jnp.concatenate([z, uv], axis=-1)
    is_even = (jnp.arange(MLA_HEADS) % 2 == 0)[:, None, None]
    w_vpair = jnp.where(is_even, even, odd).astype(BF16)
    return wa, wq1, wq2, wk, wv, w_qlat, w_vpair


def kernel(x_prompt, x_sample, cache_mla_ckv, cache_mla_kpe, page_table, state_dil_kv_w128, state_dil_kv_w512, state_dil_kv_w2048, mla_w_dq, mla_g_q, mla_w_uq, mla_w_dkv, mla_g_kv, mla_w_uk, mla_w_uv, mla_w_o, dil_w_qkv, dil_w_o, norm_mix, norm_ffn, ffn_w_up, ffn_w_down, norm_final):
    tm = 512
    xp = x_prompt.reshape(N_PROMPT, D_MODEL)
    xs = x_sample.reshape(DEC_BATCH, D_MODEL)
    tabs_p = _rope_tables(jnp.arange(SEQ))
    tabs_s = _rope_tables(jnp.full((DEC_BATCH,), PAST_LEN))
    gfin = norm_final.reshape(1, D_MODEL)
    wup = ffn_w_up.astype(BF16)
    wdn = ffn_w_down.astype(BF16)
    kpe_pool_t = jnp.transpose(cache_mla_kpe, (0, 1, 3, 2))
    states_t = [jnp.transpose(st, (0, 1, 3, 4, 5, 2))
                for st in (state_dil_kv_w128, state_dil_kv_w512, state_dil_kv_w2048)]
    ckv_p, kpe_p, ckv_s, kpe_s = [], [], [], []
    for layer in range(DEPTH):
        i = layer // 2
        gmix = norm_mix[layer].reshape(1, D_MODEL)
        if layer % 2 == 0:
            wa, wq1, wq2, wk, wv, w_qlat, w_vpair = _mla_weights(mla_w_dq[i], mla_w_uq[i], mla_w_dkv[i], mla_w_uk[i], mla_w_uv[i])
            gq = mla_g_q[i].reshape(1, MLA_Q_LORA)
            gkv = mla_g_kv[i].reshape(1, MLA_KV_LORA)
            q_p, c_p, k_p, kf_p, v_p = _mla_proj(xp, gmix, wa, gq, wq1, wq2, gkv, tabs_p, wk, wv, tm=tm, with_kv=True)
            q_s, c_s, k_s = _mla_proj(xs, gmix, wa, gq, wq1, wq2, gkv, tabs_s, None, None, tm=DEC_BATCH, with_kv=False)
            qlat_t = jnp.transpose(_mla_qlat(q_s, w_qlat), (1, 2, 0))
            qpe_t = jnp.transpose(q_s.reshape(DEC_BATCH, MLA_HEADS, LANES)[:, :, MLA_NOPE:MLA_NOPE + MLA_ROPE],
                                  (0, 2, 1)).astype(F32)
            o_p, olat, *fills = _mla_fused_attn(q_p, kf_p, v_p, page_table, qlat_t, qpe_t, c_s.T, k_s.T,
                                                cache_mla_ckv, kpe_pool_t, i, tq=512, fill=layer == 0)
            if layer == 0:
                dil_p_bufs, dil_s_bufs = fills[:len(DIL_GROUPS)], fills[len(DIL_GROUPS):]
            o_s = _mla_sample_out(jnp.transpose(olat, (1, 0, 2)), w_vpair)
            wo = mla_w_o[i].astype(BF16)
            ckv_p.append(c_p.reshape(BATCH, SEQ, MLA_KV_LORA))
            kpe_p.append(k_p)
            ckv_s.append(c_s.reshape(DEC_BATCH, 1, MLA_KV_LORA))
            kpe_s.append(k_s.reshape(DEC_BATCH, 1, MLA_ROPE))
        else:
            wqkv = dil_w_qkv[i].astype(BF16)
            q_p, kv_p, *dil_p_bufs = _dil_proj_prompt(xp, gmix, wqkv, i, dil_p_bufs, tm=256)
            q_s, kv_s, kvt_s = _dil_proj_sample(xs, gmix, wqkv)
            o_p, o_s, *dil_s_bufs = _dil_fused(q_p, kv_p, q_s, kv_s, kvt_s, states_t, i, dil_s_bufs)
            o_s = o_s.reshape(DEC_BATCH, DIL_WIDTH)
            wo = dil_w_o[i].astype(BF16)
        final = layer == DEPTH - 1
        g_ffn = norm_ffn[layer].reshape(1, D_MODEL)
        xp = _post_ffn(xp, o_p, wo, g_ffn, wup, wdn, gfin, layer, tm=tm, final=final)
        xs = _post_ffn(xs, o_s, wo, g_ffn, wup, wdn, gfin, layer, tm=DEC_BATCH, final=final)
    nl = DEPTH // 2
    to_window = lambda a, nb: jnp.transpose(
        a.reshape(nl, nb, 2, DIL_HEADS, DIL_HEAD_DIM, a.shape[-1]), (0, 1, 5, 2, 3, 4))
    return (xp.reshape(BATCH, SEQ, D_MODEL), xs.reshape(DEC_BATCH, 1, D_MODEL),
            jnp.stack(ckv_p), jnp.transpose(jnp.stack(kpe_p), (0, 1, 3, 2)), jnp.stack(ckv_s), jnp.stack(kpe_s),
            *[to_window(a, BATCH) for a in dil_p_bufs],
            *[to_window(a, DEC_BATCH) for a in dil_s_bufs])
```

```python
import functools
import math

import jax
import jax.numpy as jnp
import numpy as np
from jax import lax
from jax.experimental import pallas as pl
from jax.experimental.pallas import tpu as pltpu

F32 = jnp.float32
BF16 = jnp.bfloat16

D_MODEL = 1024
BATCH = 8
SEQ = 2048
DEPTH = 4
DEC_BATCH = 32
PAST_LEN = 16384
PAGE_SIZE = 128
N_PAGES = PAST_LEN // PAGE_SIZE
MLA_HEADS = 16
MLA_NOPE = 64
MLA_ROPE = 32
MLA_VDIM = 64
MLA_Q_LORA = 384
MLA_KV_LORA = 256
ROPE_THETA = 10000.0
DIL_GROUPS = ((128, 1), (512, 4), (2048, 16))
DIL_SPAN = 128
DIL_HEADS = 8
DIL_HEAD_DIM = 64
DIL_WIDTH = DIL_HEADS * DIL_HEAD_DIM
D_FF = 4 * D_MODEL
EPS = 1e-6
NEG_INF = -1e30
LOG2E = math.log2(math.e)

LANES = 128
N_PROMPT = BATCH * SEQ
HEAD_PAD = MLA_HEADS * LANES
VMEM_LIMIT = 56 * 1024 * 1024
DIL_FUSED_VMEM_LIMIT = 60 * 1024 * 1024
FUSED_PAGES_PER_STEP = 64
PAGES_PER_CHUNK = 2
FILL_COPIES = len(DIL_GROUPS) * (DEPTH // 2) * (2 + DEC_BATCH // BATCH)
DIL_HEAD_BLOCK = 4
NT = (((1,), (1,)), ((), ()))
TN = (((0,), (0,)), ((), ()))


def _cparams(sem):
    return pltpu.CompilerParams(dimension_semantics=sem, vmem_limit_bytes=VMEM_LIMIT)


def _rms(x, g):
    return x * lax.rsqrt(jnp.mean(x * x, axis=-1, keepdims=True) + EPS) * g


def _const_spec(shape):
    nd = len(shape)
    return pl.BlockSpec(shape, lambda *_: (0,) * nd, pipeline_mode=pl.Buffered(1))


def _any_spec():
    return pl.BlockSpec(memory_space=pl.ANY)


def _mla_proj_kernel(*refs, with_kv):
    if with_kv:
        (x_ref, gmix_ref, wa_ref, gq_ref, wq1_ref, wq2_ref, gkv_ref, cq_ref, sq_ref, ck_ref, sk_ref,
         wk_ref, wv_ref, q_out, ckv_out, kpe_out, k_out, v_out) = refs
    else:
        (x_ref, gmix_ref, wa_ref, gq_ref, wq1_ref, wq2_ref, gkv_ref, cq_ref, sq_ref, ck_ref, sk_ref,
         q_out, ckv_out, kpe_out) = refs
    h = _rms(x_ref[...], gmix_ref[...]).astype(BF16)
    a = jnp.dot(h, wa_ref[...], preferred_element_type=F32)
    c_q = _rms(a[:, :MLA_Q_LORA], gq_ref[...]).astype(BF16)
    o1 = MLA_Q_LORA + MLA_KV_LORA
    c_kv = _rms(a[:, MLA_Q_LORA:o1], gkv_ref[...])
    k_pe = a[:, o1:o1 + LANES] * ck_ref[...] + a[:, o1 + LANES:o1 + 2 * LANES] * sk_ref[...]
    ckv_out[...] = c_kv
    if with_kv:
        kpe_out[...] = k_pe.T[:MLA_ROPE, :]
    else:
        kpe_out[...] = k_pe[:, :MLA_ROPE]
    heads_per_chunk = 4
    cw = heads_per_chunk * LANES
    cq_c = jnp.concatenate([cq_ref[...]] * heads_per_chunk, axis=1)
    sq_c = jnp.concatenate([sq_ref[...]] * heads_per_chunk, axis=1)
    for c in range(MLA_HEADS // heads_per_chunk):
        q1 = jnp.dot(c_q, wq1_ref[:, c * cw:(c + 1) * cw], preferred_element_type=F32)
        q2 = jnp.dot(c_q, wq2_ref[:, c * cw:(c + 1) * cw], preferred_element_type=F32)
        q_out[:, c * cw:(c + 1) * cw] = (q1 * cq_c + q2 * sq_c).astype(q_out.dtype)
    if with_kv:
        c_kv_b = c_kv.astype(BF16)
        lhs = jnp.concatenate([c_kv_b, k_pe.astype(BF16)], axis=1)
        k_out[...] = jnp.dot(lhs, wk_ref[...], preferred_element_type=F32).astype(k_out.dtype)
        v_out[...] = jnp.dot(c_kv_b, wv_ref[...], preferred_element_type=F32).astype(v_out.dtype)


def _mla_proj(x, gmix, wa, gq, wq1, wq2, gkv, tabs, wk, wv, *, tm, with_kv):
    n = x.shape[0]
    nt = n // tm
    tab_blocks = tabs[0].shape[0] // tm
    row = lambda w: pl.BlockSpec((tm, w), lambda i: (i, 0))
    tab = pl.BlockSpec((tm, LANES), lambda i: (i % tab_blocks, 0))
    in_specs = [row(D_MODEL), _const_spec(gmix.shape), _const_spec(wa.shape), _const_spec(gq.shape),
                _const_spec(wq1.shape), _const_spec(wq2.shape), _const_spec(gkv.shape), tab, tab, tab, tab]
    args = [x, gmix, wa, gq, wq1, wq2, gkv, *tabs]
    out_shape = [jax.ShapeDtypeStruct((n, HEAD_PAD), BF16), jax.ShapeDtypeStruct((n, MLA_KV_LORA), F32)]
    out_specs = [row(HEAD_PAD), row(MLA_KV_LORA)]
    if with_kv:
        per_b = SEQ // tm
        in_specs += [_const_spec(wk.shape), _const_spec(wv.shape)]
        args += [wk, wv]
        out_shape += [jax.ShapeDtypeStruct((BATCH, MLA_ROPE, SEQ), F32),
                      jax.ShapeDtypeStruct((n, HEAD_PAD), BF16),
                      jax.ShapeDtypeStruct((n, MLA_HEADS * MLA_VDIM), BF16)]
        out_specs += [pl.BlockSpec((None, MLA_ROPE, tm), lambda i: (i // per_b, 0, i % per_b)),
                      row(HEAD_PAD), row(MLA_HEADS * MLA_VDIM)]
    else:
        out_shape.append(jax.ShapeDtypeStruct((n, MLA_ROPE), F32))
        out_specs.append(row(MLA_ROPE))
    return pl.pallas_call(
        functools.partial(_mla_proj_kernel, with_kv=with_kv),
        grid=(nt,), in_specs=in_specs, out_specs=out_specs, out_shape=out_shape,
        compiler_params=_cparams(("parallel",)), name="mla_proj_kv" if with_kv else "mla_proj_q",
    )(*args)


def _prompt_attn_body(q_ref, k_ref, v_ref, o_ref, tq):
    row = lax.broadcasted_iota(jnp.int32, (tq, tq), 0)
    col = lax.broadcasted_iota(jnp.int32, (tq, tq), 1)
    causal = col <= row
    lane = lax.broadcasted_iota(jnp.int32, (tq, LANES), 1)
    for i in reversed(range(SEQ // tq)):
        w0 = i * tq
        outs = []
        for hh in range(2):
            hs = slice(hh * LANES, (hh + 1) * LANES)
            q = q_ref[w0:w0 + tq, hs]
            s_d = lax.dot_general(q, k_ref[w0:w0 + tq, hs], NT, preferred_element_type=F32)
            s_d = jnp.where(causal, s_d, NEG_INF)
            m = jnp.max(s_d, axis=-1, keepdims=True)
            if i > 0:
                s_o = lax.dot_general(q, k_ref[0:w0, hs], NT, preferred_element_type=F32)
                m = jnp.maximum(m, jnp.max(s_o, axis=-1, keepdims=True))
            p_d = jnp.exp2(s_d - m)
            l = jnp.sum(p_d, axis=-1, keepdims=True)
            acc = jnp.dot(p_d.astype(BF16), v_ref[w0:w0 + tq, :], preferred_element_type=F32)
            if i > 0:
                p_o = jnp.exp2(s_o - m)
                l = l + jnp.sum(p_o, axis=-1, keepdims=True)
                acc = acc + jnp.dot(p_o.astype(BF16), v_ref[0:w0, :], preferred_element_type=F32)
            outs.append(acc / l)
        o_ref[w0:w0 + tq, :] = jnp.where(lane < MLA_VDIM, outs[0], outs[1]).astype(o_ref.dtype)


def _window_fill_shapes():
    nl = DEPTH // 2
    prompt = [jax.ShapeDtypeStruct((nl, BATCH, 2 * DIL_WIDTH, window), F32) for window, _ in DIL_GROUPS]
    sample = [jax.ShapeDtypeStruct((nl, DEC_BATCH, 2, DIL_HEADS, DIL_HEAD_DIM, window), F32)
              for window, _ in DIL_GROUPS]
    return prompt + sample


def _mla_qlat_kernel(q_ref, w_ref, o_ref):
    for h in range(MLA_HEADS):
        o_ref[h] = jnp.dot(q_ref[:, h * LANES:(h + 1) * LANES], w_ref[h], preferred_element_type=F32)


def _mla_qlat(q_s, w_qlat):
    return pl.pallas_call(
        _mla_qlat_kernel,
        out_shape=jax.ShapeDtypeStruct((MLA_HEADS, DEC_BATCH, MLA_KV_LORA), F32),
        compiler_params=pltpu.CompilerParams(vmem_limit_bytes=VMEM_LIMIT), name="mla_qlat",
    )(q_s, w_qlat)


def _sample_attn_init(batch, qlat_ref, qpe_ref, cnew_ref, knew_ref, cnew_row_ref, m_s, l_s, acc_s):
    pick = lambda ref: jnp.sum(
        jnp.where(lax.broadcasted_iota(jnp.int32, ref.shape, 1) == batch, ref[...], 0.0), axis=1, keepdims=True)
    s_new = (jnp.sum(qlat_ref[...] * pick(cnew_ref), axis=0, keepdims=True)
             + jnp.sum(qpe_ref[...] * pick(knew_ref), axis=0, keepdims=True))
    m_s[...] = s_new
    l_s[...] = jnp.ones_like(s_new)
    acc_s[...] = jnp.broadcast_to(cnew_row_ref[...], acc_s.shape)


def _heads_to_col(r):
    eye = (lax.broadcasted_iota(jnp.int32, (MLA_HEADS, MLA_HEADS), 0)
           == lax.broadcasted_iota(jnp.int32, (MLA_HEADS, MLA_HEADS), 1))
    return jnp.sum(jnp.where(eye, r, 0.0), axis=1, keepdims=True)


def _sample_attn_pages(qlat_ref, qpe_ref, ckv_refs, kpe_refs, m_s, l_s, acc_s, per):
    qlat_b = qlat_ref[...].astype(BF16)
    qpe_b = qpe_ref[...].astype(BF16)
    m_old = m_s[...]
    m_new = m_old
    parts = []
    for j in range(len(ckv_refs) // per):
        ckv = jnp.concatenate([r[...] for r in ckv_refs[j * per:(j + 1) * per]], axis=0).astype(BF16)
        kpe_t = jnp.concatenate([r[...] for r in kpe_refs[j * per:(j + 1) * per]], axis=1).astype(BF16)
        s = (jnp.dot(ckv, qlat_b, preferred_element_type=F32)
             + lax.dot_general(kpe_t, qpe_b, TN, preferred_element_type=F32))
        m_new = jnp.maximum(m_new, jnp.max(s, axis=0, keepdims=True))
        parts.append((s, ckv))
    alpha = jnp.exp2(m_old - m_new)
    l = alpha * l_s[...]
    acc = _heads_to_col(alpha) * acc_s[...]
    for s, ckv in parts:
        p = jnp.exp2(s - m_new)
        l = l + jnp.sum(p, axis=0, keepdims=True)
        acc = acc + lax.dot_general(p.astype(BF16), ckv, TN, preferred_element_type=F32)
    m_s[...] = m_new
    l_s[...] = l
    acc_s[...] = acc
    return l, acc


def _mla_fused_attn_kernel(pt_ref, q_ref, k_ref, v_ref, qlat_ref, qpe_ref, cnew_ref, knew_ref, cnew_row_ref,
                           *rest, tq, npg, fill):
    del pt_ref
    ckv_refs = rest[:npg]
    kpe_refs = rest[npg:2 * npg]
    rest = rest[2 * npg:]
    b = pl.program_id(0)
    j = pl.program_id(1)
    t = b * pl.num_programs(1) + j
    steps_per_row = N_PAGES // npg
    copies = []
    if fill:
        o_ref, olat_ref, *bufs, m_s, l_s, acc_s, zero_s, sems = rest

        @pl.when(t == 0)
        def _():
            zero_s[...] = jnp.zeros(zero_s.shape, zero_s.dtype)

        ng = len(DIL_GROUPS)
        per_b = DEC_BATCH // BATCH
        for g, (window, _) in enumerate(DIL_GROUPS):
            cols = pl.ds(0, window)
            for layer in range(DEPTH // 2):
                for half in range(2):
                    rows = pl.ds(j * (2 * DIL_HEAD_DIM) + half * DIL_HEAD_DIM, DIL_HEAD_DIM)
                    copies.append((zero_s.at[0, :, cols], bufs[g].at[layer, b, rows, :]))
                for bb in range(per_b):
                    copies.append((zero_s.at[:, :, cols], bufs[ng + g].at[layer, b * per_b + bb, :, j]))
        copies = [pltpu.make_async_copy(src, dst, sems.at[n]) for n, (src, dst) in enumerate(copies)]
        assert len(copies) == FILL_COPIES
        for c in copies:
            c.start()
    else:
        o_ref, olat_ref, m_s, l_s, acc_s = rest

    @pl.when(t % steps_per_row == 0)
    def _():
        _sample_attn_init(t // steps_per_row, qlat_ref, qpe_ref, cnew_ref, knew_ref, cnew_row_ref, m_s, l_s, acc_s)

    l, acc = _sample_attn_pages(qlat_ref, qpe_ref, ckv_refs, kpe_refs, m_s, l_s, acc_s, PAGES_PER_CHUNK)
    _prompt_attn_body(q_ref, k_ref, v_ref, o_ref, tq)
    for c in copies:
        c.wait()

    @pl.when(t % steps_per_row == steps_per_row - 1)
    def _():
        olat_ref[...] = acc / _heads_to_col(l)


def _mla_fused_attn(q, k, v, page_table, qlat_t, qpe_t, cnew_t, knew_t, ckv_pool, kpe_pool_t, layer, *, tq, fill):
    pairs = MLA_HEADS // 2
    npg = FUSED_PAGES_PER_STEP
    steps_per_row = N_PAGES // npg
    assert BATCH * pairs * npg == DEC_BATCH * N_PAGES and pairs == DIL_HEADS
    fill_shapes = _window_fill_shapes() if fill else []
    fill_scratch = [pltpu.VMEM((2, DIL_HEAD_DIM, max(w for w, _ in DIL_GROUPS)), F32),
                    pltpu.SemaphoreType.DMA((FILL_COPIES,))] if fill else []
    srow = lambda b, j: (b * pairs + j) // steps_per_row
    prompt = lambda w: pl.BlockSpec((SEQ, w), lambda b, j, pt: (b, j))
    per_row = lambda r, w: pl.BlockSpec((None, r, w), lambda b, j, pt: (srow(b, j), 0, 0))
    whole = lambda r: pl.BlockSpec((r, DEC_BATCH), lambda b, j, pt: (0, 0))
    page = lambda r, w, kk: pl.BlockSpec(
        (None, None, r, w),
        lambda b, j, pt, kk=kk: (layer, pt[srow(b, j), ((b * pairs + j) % steps_per_row) * npg + kk], 0, 0))
    in_specs = ([prompt(2 * LANES), prompt(2 * LANES), prompt(LANES),
                 per_row(MLA_KV_LORA, MLA_HEADS), per_row(MLA_ROPE, MLA_HEADS), whole(MLA_KV_LORA), whole(MLA_ROPE),
                 per_row(1, MLA_KV_LORA)]
                + [page(PAGE_SIZE, MLA_KV_LORA, kk) for kk in range(npg)]
                + [page(MLA_ROPE, PAGE_SIZE, kk) for kk in range(npg)])
    grid_spec = pltpu.PrefetchScalarGridSpec(
        num_scalar_prefetch=1, grid=(BATCH, pairs), in_specs=in_specs,
        out_specs=[pl.BlockSpec((SEQ, LANES), lambda b, j, pt: (b, j)),
                   pl.BlockSpec((None, MLA_HEADS, MLA_KV_LORA), lambda b, j, pt: (srow(b, j), 0, 0))]
                  + [_any_spec()] * len(fill_shapes),
        scratch_shapes=[pltpu.VMEM((1, MLA_HEADS), F32), pltpu.VMEM((1, MLA_HEADS), F32),
                        pltpu.VMEM((MLA_HEADS, MLA_KV_LORA), F32)] + fill_scratch)
    return pl.pallas_call(
        functools.partial(_mla_fused_attn_kernel, tq=tq, npg=npg, fill=fill), grid_spec=grid_spec,
        out_shape=[jax.ShapeDtypeStruct((N_PROMPT, MLA_HEADS * MLA_VDIM), BF16),
                   jax.ShapeDtypeStruct((DEC_BATCH, MLA_HEADS, MLA_KV_LORA), F32)] + fill_shapes,
        compiler_params=_cparams(("arbitrary", "arbitrary")), name="mla_attn_fused",
    )(page_table, q, k, v, qlat_t, qpe_t, cnew_t, knew_t, cnew_t.T.reshape(DEC_BATCH, 1, MLA_KV_LORA),
      *([ckv_pool] * npg), *([kpe_pool_t] * npg))


def _mla_sample_out_kernel(olat_ref, w_ref, o_ref):
    for j in range(MLA_HEADS // 2):
        acc = None
        for h in (2 * j, 2 * j + 1):
            t = jnp.dot(olat_ref[h].astype(BF16), w_ref[h], preferred_element_type=F32)
            acc = t if acc is None else acc + t
        o_ref[:, j * LANES:(j + 1) * LANES] = acc.astype(o_ref.dtype)


def _mla_sample_out(olat_t, w_vpair):
    return pl.pallas_call(
        _mla_sample_out_kernel,
        out_shape=jax.ShapeDtypeStruct((DEC_BATCH, MLA_HEADS * MLA_VDIM), BF16),
        compiler_params=pltpu.CompilerParams(vmem_limit_bytes=VMEM_LIMIT), name="mla_sample_out",
    )(olat_t, w_vpair)


def _post_ffn_kernel(*refs, final, ff_chunk):
    if final:
        x_ref, o_ref, wo_ref, g_ref, wup_ref, wdn_ref, gfin_ref, out_ref = refs
    else:
        x_ref, o_ref, wo_ref, g_ref, wup_ref, wdn_ref, out_ref = refs
    x1 = x_ref[...] + jnp.dot(o_ref[...], wo_ref[...], preferred_element_type=F32)
    hn = _rms(x1, g_ref[...]).astype(BF16)
    acc = x1
    for c in range(D_FF // ff_chunk):
        u = jnp.dot(hn, wup_ref[:, c * ff_chunk:(c + 1) * ff_chunk], preferred_element_type=F32)
        u = jnp.square(jnp.maximum(u, 0.0)).astype(BF16)
        acc = acc + jnp.dot(u, wdn_ref[c * ff_chunk:(c + 1) * ff_chunk, :], preferred_element_type=F32)
    if final:
        acc = _rms(acc, gfin_ref[...])
    out_ref[...] = acc


def _post_ffn(x, o, wo, g, wup, wdn, gfin, layer, *, tm, final):
    n = x.shape[0]
    ko = o.shape[1]
    row = lambda w: pl.BlockSpec((tm, w), lambda i: (i, 0))
    per_layer = lambda a: pl.BlockSpec((None,) + a.shape[1:], lambda i: (layer, 0, 0), pipeline_mode=pl.Buffered(1))
    in_specs = [row(D_MODEL), row(ko), _const_spec(wo.shape), _const_spec(g.shape),
                per_layer(wup), per_layer(wdn)]
    args = [x, o, wo, g, wup, wdn]
    if final:
        in_specs.append(_const_spec(gfin.shape))
        args.append(gfin)
    return pl.pallas_call(
        functools.partial(_post_ffn_kernel, final=final, ff_chunk=1024),
        grid=(n // tm,), in_specs=in_specs, out_specs=row(D_MODEL),
        out_shape=jax.ShapeDtypeStruct((n, D_MODEL), F32),
        compiler_params=_cparams(("parallel",)), name="post_ffn_final" if final else "post_ffn",
    )(*args)


def _dil_proj_prompt_kernel(*refs):
    x_ref, g_ref, w_ref = refs[:3]
    q_out, kv_out, b128_out, b512_out, b2048_out = refs[-5:]
    h = _rms(x_ref[...], g_ref[...]).astype(BF16)
    gw = 3 * DIL_WIDTH
    tm = x_ref.shape[0]
    for g, buf_out in enumerate((b128_out, b512_out, b2048_out)):
        r = jnp.dot(h, w_ref[:, g * gw:(g + 1) * gw], preferred_element_type=F32)
        q_out[g] = r[:, :DIL_WIDTH] * (DIL_HEAD_DIM ** -0.5 * LOG2E)
        kv = r[:, DIL_WIDTH:]
        kv_out[g] = kv
        kv_t = kv.T
        wb = buf_out.shape[-1]
        buf_out[...] = kv_t[:, tm - wb:] if wb < tm else kv_t


def _dil_proj_prompt(x, g, w, layer, prev, *, tm):
    n = x.shape[0]
    ng = len(DIL_GROUPS)
    per_b = SEQ // tm
    kvw = 2 * DIL_WIDTH
    buf_shapes, buf_specs = [], []
    for window, _ in DIL_GROUPS:
        wb = min(window, tm)
        nblk = window // wb
        first = per_b - nblk
        buf_shapes.append(jax.ShapeDtypeStruct((DEPTH // 2, BATCH, kvw, window), F32))
        buf_specs.append(pl.BlockSpec(
            (None, None, kvw, wb),
            lambda i, first=first: (layer, i // per_b, 0, jnp.maximum(i % per_b - first, 0))))
    in_specs = [pl.BlockSpec((tm, D_MODEL), lambda i: (i, 0)), _const_spec(g.shape), _const_spec(w.shape)]
    in_specs += [_any_spec()] * ng
    args = [x, g, w, *prev]
    aliases = {3 + k: 2 + k for k in range(ng)}
    return pl.pallas_call(
        _dil_proj_prompt_kernel, grid=(n // tm,),
        in_specs=in_specs,
        out_specs=[pl.BlockSpec((ng, tm, DIL_WIDTH), lambda i: (0, i, 0)),
                   pl.BlockSpec((ng, tm, kvw), lambda i: (0, i, 0))] + buf_specs,
        out_shape=[jax.ShapeDtypeStruct((ng, n, DIL_WIDTH), F32), jax.ShapeDtypeStruct((ng, n, kvw), F32)] + buf_shapes,
        input_output_aliases=aliases,
        compiler_params=_cparams(("arbitrary",)), name="dil_proj_prompt",
    )(*args)


def _dil_proj_sample_kernel(x_ref, g_ref, w_ref, q_out, kv_out, kvt_out):
    h = _rms(x_ref[...], g_ref[...]).astype(BF16)
    gw = 3 * DIL_WIDTH
    pad = jnp.zeros((LANES - DEC_BATCH, 2 * DIL_WIDTH), F32)
    for g in range(len(DIL_GROUPS)):
        r = jnp.dot(h, w_ref[:, g * gw:(g + 1) * gw], preferred_element_type=F32)
        q_out[g] = r[:, :DIL_WIDTH] * (DIL_HEAD_DIM ** -0.5)
        kv = r[:, DIL_WIDTH:]
        kv_out[g] = kv
        kvt_out[g] = jnp.concatenate([kv, pad], axis=0).T


def _dil_proj_sample(x, g, w):
    ng = len(DIL_GROUPS)
    return pl.pallas_call(
        _dil_proj_sample_kernel,
        out_shape=[jax.ShapeDtypeStruct((ng, DEC_BATCH, DIL_WIDTH), F32),
                   jax.ShapeDtypeStruct((ng, DEC_BATCH, 2 * DIL_WIDTH), F32),
                   jax.ShapeDtypeStruct((ng, 2 * DIL_WIDTH, LANES), F32)],
        compiler_params=pltpu.CompilerParams(vmem_limit_bytes=VMEM_LIMIT), name="dil_proj_sample",
    )(x, g, w)


def _strided(start, size, stride):
    return pl.ds(start, size) if stride == 1 else pl.ds(start, size, stride=stride)


def _dil_prompt_combos(q_ref, k_ref, v_ref, os_ref, ms_ref, ds_ref, j, work):
    sp = DIL_SPAN
    lane = lax.broadcasted_iota(jnp.int32, (sp, LANES), 1)
    first_head = lane < DIL_HEAD_DIM
    jq = lax.broadcasted_iota(jnp.int32, (sp, 2 * sp), 0)
    jk = lax.broadcasted_iota(jnp.int32, (sp, 2 * sp), 1)
    dist = jq + sp - jk
    valid = (dist >= 0) & (dist <= sp)
    slopes = [LOG2E * jnp.exp2(-(8.0 / DIL_HEADS) * (2 * j + hh + 1).astype(F32)) for hh in range(2)]
    sel = lambda a: jnp.where(first_head, a[:sp], a[sp:])
    wide = lambda a: jnp.broadcast_to(a, (2 * sp, LANES))
    own_lanes = ((lax.broadcasted_iota(jnp.int32, (2 * sp, LANES), 1) < DIL_HEAD_DIM)
                 == (lax.broadcasted_iota(jnp.int32, (2 * sp, LANES), 0) < sp))
    for g, residues in work:
        d = DIL_GROUPS[g][1]
        dist_f = (dist * d).astype(F32)
        bias_full = jnp.concatenate([jnp.where(valid, -slopes[hh] * dist_f, NEG_INF) for hh in range(2)], axis=0)
        bias_cur = bias_full[:, sp:]
        for r in residues:
            k_prev = v_prev = None
            for n in range(SEQ // (sp * d)):
                qs = n * sp * d + r
                rows = _strided(qs, sp, d)
                q = q_ref[g, rows, :]
                k_cur = k_ref[g, rows, :].astype(BF16)
                v_cur = v_ref[g, rows, :].astype(BF16)
                k = k_cur if n == 0 else jnp.concatenate([k_prev, k_cur], axis=0)
                v = v_cur if n == 0 else jnp.concatenate([v_prev, v_cur], axis=0)
                k_prev, v_prev = k_cur, v_cur
                q2 = jnp.where(own_lanes, jnp.concatenate([q, q], axis=0), 0.0).astype(BF16)
                s = lax.dot_general(q2, k, NT, preferred_element_type=F32)
                s = s + (bias_full if n > 0 else bias_cur)
                m = jnp.max(s, axis=-1, keepdims=True)
                e = jnp.exp2(s - m)
                den = jnp.sum(e, axis=-1, keepdims=True)
                o = jnp.dot(e.astype(BF16), v, preferred_element_type=F32)
                os_ref[g, rows, :] = sel(o)
                ms_ref[g, rows, :] = sel(wide(m))
                ds_ref[g, rows, :] = sel(wide(den))
def _dil_prompt_merge(o_ref, os_ref, ms_ref, ds_ref):
    m_all = jnp.maximum(jnp.maximum(ms_ref[0], ms_ref[1]), ms_ref[2])
    num = jnp.zeros((SEQ, LANES), F32)
    den = jnp.zeros((SEQ, LANES), F32)
    for g in range(len(DIL_GROUPS)):
        w = jnp.exp2(ms_ref[g] - m_all)
        num = num + os_ref[g] * w
        den = den + ds_ref[g] * w
    o_ref[...] = (num / den).astype(o_ref.dtype)


def _dil_sample_body(b, hb, q_ref, kr_ref, vr_ref, kc_ref, vc_ref, s_refs, o_ref, n_refs):
    s0_ref, s1_ref, s2_ref = s_refs
    n0_ref, n1_ref, n2_ref = n_refs
    rw = DIL_HEAD_BLOCK * DIL_HEAD_DIM
    rowi = lax.broadcasted_iota(jnp.int32, (8, rw), 0)
    coli = lax.broadcasted_iota(jnp.int32, (8, rw), 1)
    bd = (coli // DIL_HEAD_DIM) == rowi
    head = hb * DIL_HEAD_BLOCK + lax.broadcasted_iota(jnp.int32, (8, 1), 0)
    slope = jnp.exp2(-(8.0 / DIL_HEADS) * (head + 1).astype(F32))
    is_b = lax.broadcasted_iota(jnp.int32, (rw, LANES), 1) == b
    parts = []
    for g, (s_ref, n_ref, (window, d)) in enumerate(zip((s0_ref, s1_ref, s2_ref), (n0_ref, n1_ref, n2_ref), DIL_GROUPS)):
        q_bd = jnp.where(bd, q_ref[g], 0.0)
        k_old = s_ref[0].reshape(rw, window)
        v_old = s_ref[1].reshape(rw, window)
        s = jnp.dot(q_bd.astype(BF16), k_old.astype(BF16), preferred_element_type=F32)
        pos = lax.broadcasted_iota(jnp.int32, (8, window), 1)
        back = (window - pos).astype(F32)
        s = jnp.where((pos & (d - 1)) == 0, s - slope * back, NEG_INF)
        s_new = jnp.sum(q_bd * kr_ref[g], axis=1, keepdims=True)
        m = jnp.maximum(jnp.max(s, axis=1, keepdims=True), s_new)
        e = jnp.exp(s - m)
        e_new = jnp.exp(s_new - m)
        den = jnp.sum(e, axis=1, keepdims=True) + e_new
        o = (lax.dot_general(e.astype(BF16), v_old.astype(BF16), NT, preferred_element_type=F32)
             + e_new * jnp.where(bd, vr_ref[g], 0.0))
        parts.append((o, m, den))
        last = lax.broadcasted_iota(jnp.int32, (rw, window), 1) == window - 1
        for t, (old, c_ref) in enumerate(((k_old, kc_ref), (v_old, vc_ref))):
            col = jnp.sum(jnp.where(is_b, c_ref[g], 0.0), axis=1, keepdims=True)
            new = jnp.where(last, col, pltpu.roll(old, window - 1, 1))
            n_ref[t] = new.reshape(DIL_HEAD_BLOCK, DIL_HEAD_DIM, window)
    m_all = jnp.maximum(jnp.maximum(parts[0][1], parts[1][1]), parts[2][1])
    num = jnp.zeros((8, rw), F32)
    den = jnp.zeros((8, 1), F32)
    for o, m, dn in parts:
        wgt = jnp.exp(m - m_all)
        num = num + o * wgt
        den = den + dn * wgt
    o_ref[...] = jnp.sum(jnp.where(bd, num / den, 0.0), axis=0, keepdims=True).astype(o_ref.dtype)


def _dil_fused_kernel(*refs):
    q_ref, k_ref, v_ref, qs_ref, kr_ref, vr_ref, kc_ref, vc_ref = refs[:8]
    s_refs = refs[8:11]
    o_ref, os_out = refs[14:16]
    n_refs = refs[16:19]
    os_ref, ms_ref, ds_ref = refs[19:]
    b, j, h = pl.program_id(0), pl.program_id(1), pl.program_id(2)
    row = b * pl.num_programs(1) + j
    ng = len(DIL_GROUPS)
    last_d = DIL_GROUPS[ng - 1][1]
    first = [(0, range(DIL_GROUPS[0][1])), (ng - 1, range(last_d // 2))]
    second = [(g, range(DIL_GROUPS[g][1])) for g in range(1, ng - 1)] + [(ng - 1, range(last_d // 2, last_d))]

    @pl.when(h == 0)
    def _():
        _dil_prompt_combos(q_ref, k_ref, v_ref, os_ref, ms_ref, ds_ref, j, first)
        _dil_sample_body(row, h, qs_ref, kr_ref, vr_ref, kc_ref, vc_ref, s_refs, os_out, n_refs)

    @pl.when(h == 1)
    def _():
        _dil_prompt_combos(q_ref, k_ref, v_ref, os_ref, ms_ref, ds_ref, j, second)
        _dil_prompt_merge(o_ref, os_ref, ms_ref, ds_ref)
        _dil_sample_body(row, h, qs_ref, kr_ref, vr_ref, kc_ref, vc_ref, s_refs, os_out, n_refs)


def _dil_fused(q, kv, q_s, kv_s, kvt_s, states_t, layer, prev):
    ng = len(DIL_GROUPS)
    w = DIL_WIDTH
    pairs = DIL_HEADS // 2
    hbk = DIL_HEAD_BLOCK
    rw = hbk * DIL_HEAD_DIM
    nhb = DIL_HEADS // hbk
    assert BATCH * pairs == DEC_BATCH and nhb == 2
    srow = lambda b, j: b * pairs + j
    blk = lambda off: pl.BlockSpec((ng, SEQ, LANES), lambda b, j, h, off=off: (0, b, off + j))
    row = lambda off: pl.BlockSpec((ng, None, 1, rw), lambda b, j, h, off=off: (0, srow(b, j), 0, off + h))
    colspec = lambda off: pl.BlockSpec((ng, rw, LANES), lambda b, j, h, off=off: (0, off + h, 0))
    st_spec = lambda window: pl.BlockSpec((None, None, 2, hbk, DIL_HEAD_DIM, window),
                                          lambda b, j, h: (layer, srow(b, j), 0, h, 0, 0))
    st_specs = [st_spec(window) for window, _ in DIL_GROUPS]
    in_specs = ([blk(0), blk(0), blk(pairs), row(0), row(0), row(nhb), colspec(0), colspec(nhb)]
                + st_specs + [_any_spec()] * ng)
    args = [q, kv, kv, q_s.reshape(ng, DEC_BATCH, 1, w), kv_s.reshape(ng, DEC_BATCH, 1, 2 * w),
            kv_s.reshape(ng, DEC_BATCH, 1, 2 * w), kvt_s, kvt_s, *states_t, *prev]
    aliases = {len(args) - ng + k: 2 + k for k in range(ng)}
    return pl.pallas_call(
        _dil_fused_kernel, grid=(BATCH, pairs, nhb),
        in_specs=in_specs,
        out_specs=[pl.BlockSpec((SEQ, LANES), lambda b, j, h: (b, j)),
                   pl.BlockSpec((None, 1, rw), lambda b, j, h: (srow(b, j), 0, h))] + st_specs,
        out_shape=[jax.ShapeDtypeStruct((N_PROMPT, w), BF16), jax.ShapeDtypeStruct((DEC_BATCH, 1, w), BF16)]
                  + [jax.ShapeDtypeStruct(st.shape, F32) for st in states_t],
        scratch_shapes=[pltpu.VMEM((ng, SEQ, LANES), F32)] * 3,
        input_output_aliases=aliases,
        compiler_params=pltpu.CompilerParams(dimension_semantics=("arbitrary",) * 3,
                                             vmem_limit_bytes=DIL_FUSED_VMEM_LIMIT),
        name="dil_attn_fused",
    )(*args)


def _rope_tables(pos):
    half = MLA_ROPE // 2
    f32 = np.float32
    inv = f32(ROPE_THETA) ** (-np.arange(half, dtype=f32) / f32(half))
    ang = (np.asarray(pos, f32)[:, None] * inv[None, :]).astype(f32)
    cos, sin = np.cos(ang).astype(f32), np.sin(ang).astype(f32)
    n = ang.shape[0]
    scale = f32((MLA_NOPE + MLA_ROPE) ** -0.5 * LOG2E)
    zq = np.zeros((n, LANES - MLA_NOPE - MLA_ROPE), f32)
    cq = np.concatenate([np.ones((n, MLA_NOPE), f32), cos, cos, zq], axis=1) * scale
    sq = np.concatenate([np.zeros((n, MLA_NOPE), f32), -sin, sin, zq], axis=1) * scale
    zk = np.zeros((n, LANES - MLA_ROPE), f32)
    ck = np.concatenate([cos, cos, zk], axis=1)
    sk = np.concatenate([-sin, sin, zk], axis=1)
    return tuple(jnp.asarray(t) for t in (cq, sq, ck, sk))


def _mla_weights(w_dq, w_uq, w_dkv, w_uk, w_uv):
    half = MLA_ROPE // 2
    padl = lambda a, wdt: jnp.pad(a, [(0, 0)] * (a.ndim - 1) + [(0, wdt - a.shape[-1])])
    c, r = w_dkv[:, :MLA_KV_LORA], w_dkv[:, MLA_KV_LORA:]
    r_sw = jnp.concatenate([r[:, half:], r[:, :half]], axis=1)
    wa = jnp.concatenate([w_dq, c, padl(r, LANES), padl(r_sw, LANES)], axis=1).astype(BF16)
    uq = w_uq.reshape(MLA_Q_LORA, MLA_HEADS, MLA_NOPE + MLA_ROPE)
    wq1 = padl(uq, LANES).reshape(MLA_Q_LORA, HEAD_PAD).astype(BF16)
    uq_sw = jnp.concatenate([jnp.zeros_like(uq[..., :MLA_NOPE]), uq[..., MLA_NOPE + half:],
                             uq[..., MLA_NOPE:MLA_NOPE + half]], axis=-1)
    wq2 = padl(uq_sw, LANES).reshape(MLA_Q_LORA, HEAD_PAD).astype(BF16)
    wk_top = padl(w_uk, LANES).reshape(MLA_KV_LORA, HEAD_PAD)
    place = jnp.pad(jnp.eye(MLA_ROPE, dtype=F32), ((0, LANES - MLA_ROPE), (MLA_NOPE, LANES - MLA_NOPE - MLA_ROPE)))
    wk = jnp.concatenate([wk_top, jnp.tile(place, (1, MLA_HEADS))], axis=0).astype(BF16)
    wv = w_uv.reshape(MLA_KV_LORA, MLA_HEADS * MLA_VDIM).astype(BF16)
    w_qlat = padl(jnp.transpose(w_uk, (1, 0, 2)), LANES)
    w_qlat = jnp.transpose(w_qlat, (0, 2, 1)).astype(BF16)
    uv = jnp.transpose(w_uv, (1, 0, 2))
    z = jnp.zeros_like(uv)
    even = jnp.concatenate([uv, z], axis=-1)
    odd = jnp.concatenate([z, uv], axis=-1)
    is_even = (jnp.arange(MLA_HEADS) % 2 == 0)[:, None, None]
    w_vpair = jnp.where(is_even, even, odd).astype(BF16)
    return wa, wq1, wq2, wk, wv, w_qlat, w_vpair


def kernel(x_prompt, x_sample, cache_mla_ckv, cache_mla_kpe, page_table, state_dil_kv_w128, state_dil_kv_w512, state_dil_kv_w2048, mla_w_dq, mla_g_q, mla_w_uq, mla_w_dkv, mla_g_kv, mla_w_uk, mla_w_uv, mla_w_o, dil_w_qkv, dil_w_o, norm_mix, norm_ffn, ffn_w_up, ffn_w_down, norm_final):
    tm = 512
    xp = x_prompt.reshape(N_PROMPT, D_MODEL)
    xs = x_sample.reshape(DEC_BATCH, D_MODEL)
    tabs_p = _rope_tables(np.arange(SEQ))
    tabs_s = _rope_tables(np.full((DEC_BATCH,), PAST_LEN))
    gfin = norm_final.reshape(1, D_MODEL)
    wup = ffn_w_up.astype(BF16)
    wdn = ffn_w_down.astype(BF16)
    kpe_pool_t = jnp.transpose(cache_mla_kpe, (0, 1, 3, 2))
    states_t = [jnp.transpose(st, (0, 1, 3, 4, 5, 2))
                for st in (state_dil_kv_w128, state_dil_kv_w512, state_dil_kv_w2048)]
    ckv_p, kpe_p, ckv_s, kpe_s = [], [], [], []
    for layer in range(DEPTH):
        i = layer // 2
        gmix = norm_mix[layer].reshape(1, D_MODEL)
        if layer % 2 == 0:
            wa, wq1, wq2, wk, wv, w_qlat, w_vpair = _mla_weights(mla_w_dq[i], mla_w_uq[i], mla_w_dkv[i], mla_w_uk[i], mla_w_uv[i])
            gq = mla_g_q[i].reshape(1, MLA_Q_LORA)
            gkv = mla_g_kv[i].reshape(1, MLA_KV_LORA)
            q_p, c_p, k_p, kf_p, v_p = _mla_proj(xp, gmix, wa, gq, wq1, wq2, gkv, tabs_p, wk, wv, tm=tm, with_kv=True)
            q_s, c_s, k_s = _mla_proj(xs, gmix, wa, gq, wq1, wq2, gkv, tabs_s, None, None, tm=DEC_BATCH, with_kv=False)
            qlat_t = jnp.transpose(_mla_qlat(q_s, w_qlat), (1, 2, 0))
            qpe_t = jnp.transpose(q_s.reshape(DEC_BATCH, MLA_HEADS, LANES)[:, :, MLA_NOPE:MLA_NOPE + MLA_ROPE],
                                  (0, 2, 1)).astype(F32)
            o_p, olat, *fills = _mla_fused_attn(q_p, kf_p, v_p, page_table, qlat_t, qpe_t, c_s.T, k_s.T,
                                                cache_mla_ckv, kpe_pool_t, i, tq=512, fill=layer == 0)
            if layer == 0:
                dil_p_bufs, dil_s_bufs = fills[:len(DIL_GROUPS)], fills[len(DIL_GROUPS):]
            o_s = _mla_sample_out(jnp.transpose(olat, (1, 0, 2)), w_vpair)
            wo = mla_w_o[i].astype(BF16)
            ckv_p.append(c_p.reshape(BATCH, SEQ, MLA_KV_LORA))
            kpe_p.append(k_p)
            ckv_s.append(c_s.reshape(DEC_BATCH, 1, MLA_KV_LORA))
            kpe_s.append(k_s.reshape(DEC_BATCH, 1, MLA_ROPE))
        else:
            wqkv = dil_w_qkv[i].astype(BF16)
            q_p, kv_p, *dil_p_bufs = _dil_proj_prompt(xp, gmix, wqkv, i, dil_p_bufs, tm=512)
            q_s, kv_s, kvt_s = _dil_proj_sample(xs, gmix, wqkv)
            o_p, o_s, *dil_s_bufs = _dil_fused(q_p, kv_p, q_s, kv_s, kvt_s, states_t, i, dil_s_bufs)
            o_s = o_s.reshape(DEC_BATCH, DIL_WIDTH)
            wo = dil_w_o[i].astype(BF16)
        final = layer == DEPTH - 1
        g_ffn = norm_ffn[layer].reshape(1, D_MODEL)
        xp = _post_ffn(xp, o_p, wo, g_ffn, wup, wdn, gfin, layer, tm=tm, final=final)
        xs = _post_ffn(xs, o_s, wo, g_ffn, wup, wdn, gfin, layer, tm=DEC_BATCH, final=final)
    nl = DEPTH // 2
    to_window = lambda a, nb: jnp.transpose(
        a.reshape(nl, nb, 2, DIL_HEADS, DIL_HEAD_DIM, a.shape[-1]), (0, 1, 5, 2, 3, 4))
    return (xp.reshape(BATCH, SEQ, D_MODEL), xs.reshape(DEC_BATCH, 1, D_MODEL),
            jnp.stack(ckv_p), jnp.transpose(jnp.stack(kpe_p), (0, 1, 3, 2)), jnp.stack(ckv_s), jnp.stack(kpe_s),
            *[to_window(a, BATCH) for a in dil_p_bufs],
            *[to_window(a, DEC_BATCH) for a in dil_s_bufs])
```
